```python
import math
import jax, jax.numpy as jnp
from jax import lax
import numpy as np

D_MODEL = 1024
BATCH = 16
SEQ = 4096
DEPTH = 4

N_MIXERS = 2
N_EVEN = (DEPTH + 1) // 2
N_ODD = DEPTH // 2
N_MOD = 6
POOL_WINDOWS = (2, 4, 8, 16)
N_POOL_GROUPS = len(POOL_WINDOWS)
POOL_GROUP_DIM = D_MODEL // N_POOL_GROUPS
N_HEADS = 8
HEAD_DIM = D_MODEL // (2 * N_HEADS)
V_HEAD_DIM = 2 * HEAD_DIM
LAMBDA_VEC_INIT_STD = 0.1
N_BUCKETS = 32
MAX_DISTANCE = 128
Q_BLOCK = 128
D_FF = ((8 * D_MODEL // 3 + 255) // 256) * 256
N_EXPERTS = 8
TOP_K = 2
D_FF_EXPERT = 7 * D_MODEL // 2
EPS = 1e-6

kernel_name = "hybrid_pool_diffattn_moe_trunk"


def rms_norm(x, g):
    xf = x.astype(jnp.float32)
    y = xf * lax.rsqrt(jnp.mean(xf * xf, axis=-1, keepdims=True) + EPS)
    return (y * g.astype(jnp.float32)).astype(x.dtype)


def pool_mixer(h, w_grp, scale):
    B, S, D = h.shape
    hf = h.reshape(B, S, N_POOL_GROUPS, POOL_GROUP_DIM).astype(jnp.float32)
    cs = jnp.cumsum(hf, axis=1)
    count = jnp.arange(1, S + 1, dtype=jnp.float32)
    outs = []
    for g, w in enumerate(POOL_WINDOWS):
        csg = cs[:, :, g]
        lag = jnp.pad(csg, ((0, 0), (w, 0), (0, 0)))[:, :S]
        mean = (csg - lag) / jnp.minimum(count, float(w))[None, :, None]
        outs.append(mean - hf[:, :, g])
    pooled = jnp.stack(outs, axis=2).astype(h.dtype)
    mixed = jnp.einsum('bsgc,gce->bsge', pooled, w_grp)
    return mixed.reshape(B, S, D) * scale


def rel_bucket(q_pos, k_pos):
    n = jnp.maximum(q_pos[:, None] - k_pos[None, :], 0)
    max_exact = N_BUCKETS // 2
    nf = jnp.maximum(n, 1).astype(jnp.float32)
    large = max_exact + (jnp.log(nf / max_exact) / math.log(MAX_DISTANCE / max_exact)
                         * (N_BUCKETS - max_exact)).astype(jnp.int32)
    large = jnp.minimum(large, N_BUCKETS - 1)
    return jnp.where(n < max_exact, n, large)


def diff_attention(h, w_qkv, w_o, subln_g, lam_vecs, rel_bias, lambda_init):
    B, S, D = h.shape
    q, k, v = jnp.split(h @ w_qkv, 3, axis=-1)
    q = q.reshape(B, S, N_HEADS, 2, HEAD_DIM) * (HEAD_DIM ** -0.5)
    k = k.reshape(B, S, N_HEADS, 2, HEAD_DIM)
    v = v.reshape(B, S, N_HEADS, V_HEAD_DIM)
    lv = lam_vecs.astype(jnp.float32)
    lam = jnp.exp(jnp.sum(lv[0] * lv[1])) - jnp.exp(jnp.sum(lv[2] * lv[3])) + lambda_init
    n_blk = S // Q_BLOCK
    q_blocks = q.reshape(B, n_blk, Q_BLOCK, N_HEADS, 2, HEAD_DIM).transpose(1, 0, 2, 3, 4, 5)
    k_pos = jnp.arange(S)

    def attend(args):
        q_blk, blk = args
        q_pos = blk * Q_BLOCK + jnp.arange(Q_BLOCK)
        bias = rel_bias[rel_bucket(q_pos, k_pos)].astype(jnp.float32).transpose(2, 0, 1)
        s = jnp.einsum('bqhcd,bkhcd->bhcqk', q_blk, k).astype(jnp.float32) + bias[None, :, None]
        s = jnp.where(k_pos[None, :] <= q_pos[:, None], s, -jnp.inf)
        p = jax.nn.softmax(s, axis=-1)
        a = p[:, :, 0] - lam * p[:, :, 1]
        return jnp.einsum('bhqk,bkhe->bqhe', a.astype(v.dtype), v)

    o = lax.map(attend, (q_blocks, jnp.arange(n_blk)))
    o = o.transpose(1, 0, 2, 3, 4).reshape(B, S, N_HEADS, V_HEAD_DIM)
    o = rms_norm(o, subln_g) * (1.0 - lambda_init)
    return o.reshape(B, S, D) @ w_o


def swiglu(t, w_gu, w_down):
    g, u = jnp.split(t @ w_gu, 2, axis=-1)
    return (jax.nn.silu(g) * u) @ w_down


def moe_swiglu(h, w_router, w_gu, w_down):
    B, S, D = h.shape
    t = h.reshape(-1, D)
    logits = (t @ w_router).astype(jnp.float32)
    top_val, top_idx = lax.top_k(logits, TOP_K)
    top_w = jax.nn.softmax(top_val, axis=-1)
    gates = jnp.sum(jax.nn.one_hot(top_idx, N_EXPERTS, dtype=jnp.float32) * top_w[..., None], axis=1)
    out = jnp.zeros_like(t)
    for e in range(N_EXPERTS):
        out = out + gates[:, e:e + 1].astype(t.dtype) * swiglu(t, w_gu[e], w_down[e])
    return out.reshape(B, S, D)


def setup_inputs(seed: int = 0) -> dict:
    key = jax.random.key(seed)
    ks = jax.random.split(key, 17)
    f32 = jnp.float32
    D = D_MODEL
    nrm = lambda k, shape, fan_in: jax.random.normal(k, shape, f32) * (fan_in ** -0.5)
    return {
        "x": jax.random.normal(ks[0], (BATCH, SEQ, D), f32),
        "c": jax.random.normal(ks[1], (BATCH, D), f32),
        "w_mod": nrm(ks[2], (DEPTH, D, N_MOD * D), D) * 0.5,
        "b_mod": 0.01 * jax.random.normal(ks[3], (DEPTH, N_MOD * D), f32),
        "norm_g": 1.0 + 0.05 * jax.random.normal(ks[4], (DEPTH, 4, D), f32),
        "pool_w": nrm(ks[5], (N_EVEN, N_POOL_GROUPS, POOL_GROUP_DIM, POOL_GROUP_DIM), POOL_GROUP_DIM),
        "pool_scale": 1.0 + 0.1 * jax.random.normal(ks[6], (N_EVEN, D), f32),
        "w_qkv": nrm(ks[7], (N_ODD, D, 3 * D), D),
        "w_o": nrm(ks[8], (N_ODD, D, D), D),
        "subln_g": 1.0 + 0.05 * jax.random.normal(ks[9], (N_ODD, V_HEAD_DIM), f32),
        "lambda_vecs": LAMBDA_VEC_INIT_STD * jax.random.normal(ks[10], (N_ODD, 4, HEAD_DIM), f32),
        "rel_bias": 0.5 * jax.random.normal(ks[11], (N_BUCKETS, N_HEADS), f32),
        "ffn_w_gu": nrm(ks[12], (N_EVEN, D, 2 * D_FF), D),
        "ffn_w_down": nrm(ks[13], (N_EVEN, D_FF, D), D_FF),
        "w_router": nrm(ks[14], (N_ODD, D, N_EXPERTS), D),
        "moe_w_gu": nrm(ks[15], (N_ODD, N_EXPERTS, D, 2 * D_FF_EXPERT), D),
        "moe_w_down": nrm(ks[16], (N_ODD, N_EXPERTS, D_FF_EXPERT, D), D_FF_EXPERT),
    }


def reference(x, c, w_mod, b_mod, norm_g, pool_w, pool_scale, w_qkv, w_o, subln_g,
              lambda_vecs, rel_bias, ffn_w_gu, ffn_w_down, w_router, moe_w_gu, moe_w_down):
    mod_all = jnp.einsum('bd,lde->lbe', jax.nn.silu(c), w_mod) + b_mod[:, None]
    for i in range(DEPTH):
        j = i // 2
        shift1, scale1, gate1, shift2, scale2, gate2 = jnp.split(mod_all[i], N_MOD, axis=-1)
        h = rms_norm(x, norm_g[i, 0]) * (1.0 + scale1[:, None]) + shift1[:, None]
        if i % N_MIXERS == 0:
            y = pool_mixer(h, pool_w[j], pool_scale[j])
        else:
            lambda_init = 0.8 - 0.6 * math.exp(-0.3 * i)
            y = diff_attention(h, w_qkv[j], w_o[j], subln_g[j], lambda_vecs[j], rel_bias, lambda_init)
        x = x + gate1[:, None] * rms_norm(y, norm_g[i, 1])
        h = rms_norm(x, norm_g[i, 2]) * (1.0 + scale2[:, None]) + shift2[:, None]
        if i % 2 == 0:
            y = swiglu(h, ffn_w_gu[j], ffn_w_down[j])
        else:
            y = moe_swiglu(h, w_router[j], moe_w_gu[j], moe_w_down[j])
        x = x + gate2[:, None] * rms_norm(y, norm_g[i, 3])
    return x
```

```python
import functools
import math

import jax
import jax.numpy as jnp
from jax import lax
from jax.experimental import pallas as pl
from jax.experimental.pallas import tpu as pltpu

F32 = jnp.float32
BF16 = jnp.bfloat16

EPS = 1e-6
N_MOD = 6
POOL_WINDOWS = (2, 4, 8, 16)
POOL_HALO = 16
N_HEADS = 8
HEAD_DIM = 64
V_HEAD_DIM = 2 * HEAD_DIM
N_BUCKETS = 32
MAX_DISTANCE = 128
N_EXPERTS = 8
LANES = 128
NEG_BIG = -1e30
VMEM_LIMIT = 56 * 1024 * 1024

SHIFT1, SCALE1, GATE1, SHIFT2, SCALE2, GATE2 = range(N_MOD)


def _params(*sem):
    return pltpu.CompilerParams(dimension_semantics=sem, vmem_limit_bytes=VMEM_LIMIT)


def _rms(x, g):
    return x * lax.rsqrt(jnp.mean(x * x, axis=-1, keepdims=True) + EPS) * g


def _norm_mod(x, g, scale, shift):
    return _rms(x, g) * (1.0 + scale) + shift


def _silu(x):
    return x * (1.0 / (1.0 + jnp.exp(-x)))


def _row(a, i):
    return a[i:i + 1, :]


def _mod_kernel(c_ref, w_ref, b_ref, o_ref):
    c = _silu(c_ref[...])
    o_ref[0] = jnp.dot(c, w_ref[0], precision=lax.Precision.HIGHEST,
                       preferred_element_type=F32) + b_ref[0]


def _modulation(c, w_mod, b_mod):
    depth, d, e = w_mod.shape
    b = c.shape[0]
    nc = 1536
    out = pl.pallas_call(
        _mod_kernel,
        grid=(depth, e // nc),
        in_specs=[pl.BlockSpec((b, d), lambda l, j: (0, 0)),
                  pl.BlockSpec((1, d, nc), lambda l, j: (l, 0, j)),
                  pl.BlockSpec((1, 1, nc), lambda l, j: (l, 0, j))],
        out_specs=pl.BlockSpec((1, b, nc), lambda l, j: (l, 0, j)),
        out_shape=jax.ShapeDtypeStruct((depth, b, e), F32),
        compiler_params=_params("parallel", "parallel"),
    )(c, w_mod, b_mod.reshape(depth, 1, e))
    return out.reshape(depth, b, N_MOD, d)


def _pool_kernel(x_ref, mod_ref, ng_ref, pw_ref, ps_ref, o_ref, hbuf, *, ts):
    s = pl.program_id(1)
    x = x_ref[0]
    m = mod_ref[0, 0]
    ng = ng_ref[0]
    d = x.shape[-1]
    cg = d // len(POOL_WINDOWS)
    h = _norm_mod(x, _row(ng, 0), _row(m, SCALE1), _row(m, SHIFT1))

    @pl.when(s == 0)
    def _():
        hbuf[0:POOL_HALO, :] = jnp.zeros((POOL_HALO, d), F32)

    @pl.when(s > 0)
    def _():
        hbuf[0:POOL_HALO, :] = hbuf[ts:ts + POOL_HALO, :]

    hbuf[POOL_HALO:POOL_HALO + ts, :] = h
    count = s * ts + lax.broadcasted_iota(jnp.int32, (ts, 1), 0) + 1
    outs = []
    for g, w in enumerate(POOL_WINDOWS):
        lo = g * cg
        hg = h[:, lo:lo + cg]
        acc = hg
        for k in range(1, w):
            acc = acc + hbuf[POOL_HALO - k:POOL_HALO - k + ts, lo:lo + cg]
        pooled = acc / jnp.minimum(count, w).astype(F32) - hg
        outs.append(jnp.dot(pooled.astype(BF16), pw_ref[0, g], preferred_element_type=F32))
    y = jnp.concatenate(outs, axis=-1) * ps_ref[0]
    o_ref[0] = x + _row(m, GATE1) * _rms(y, _row(ng, 1))


def _pool_layer(x, mod, norm_g, pool_w, pool_scale, layer, j):
    b, s, d = x.shape
    ts = min(512, s)
    g, cg, _ = pool_w.shape[1:]
    return pl.pallas_call(
        functools.partial(_pool_kernel, ts=ts),
        grid=(b, s // ts),
        in_specs=[pl.BlockSpec((1, ts, d), lambda i, t: (i, t, 0)),
                  pl.BlockSpec((1, 1, N_MOD, d), lambda i, t: (layer, i, 0, 0)),
                  pl.BlockSpec((1, 4, d), lambda i, t: (layer, 0, 0)),
                  pl.BlockSpec((1, g, cg, cg), lambda i, t: (j, 0, 0, 0)),
                  pl.BlockSpec((1, 1, d), lambda i, t: (j, 0, 0))],
        out_specs=pl.BlockSpec((1, ts, d), lambda i, t: (i, t, 0)),
        out_shape=jax.ShapeDtypeStruct(x.shape, F32),
        scratch_shapes=[pltpu.VMEM((POOL_HALO + ts, d), F32)],
        compiler_params=_params("arbitrary", "arbitrary"),
    )(x, mod, norm_g, pool_w, pool_scale.reshape(-1, 1, d))


def _ffn_kernel(x_ref, mod_ref, ng_ref, wgu_ref, wd_ref, o_ref, *, f, fc):
    x = x_ref[0]
    m = mod_ref[0, 0]
    ng = ng_ref[0]
    h = _norm_mod(x, _row(ng, 2), _row(m, SCALE2), _row(m, SHIFT2)).astype(BF16)
    acc = jnp.zeros(x.shape, F32)
    for c in range(f // fc):
        gate = jnp.dot(h, wgu_ref[0, :, c * fc:(c + 1) * fc], preferred_element_type=F32)
        up = jnp.dot(h, wgu_ref[0, :, f + c * fc:f + (c + 1) * fc], preferred_element_type=F32)
        hm = (_silu(gate) * up).astype(BF16)
        acc = acc + jnp.dot(hm, wd_ref[0, c * fc:(c + 1) * fc, :], preferred_element_type=F32)
    o_ref[0] = x + _row(m, GATE2) * _rms(acc, _row(ng, 3))


def _ffn_layer(x, mod, norm_g, w_gu, w_down, layer, j):
    b, s, d = x.shape
    f = w_down.shape[1]
    tm = min(512, s)
    fc = 256 if f % 256 == 0 else f
    return pl.pallas_call(
        functools.partial(_ffn_kernel, f=f, fc=fc),
        grid=(b, s // tm),
        in_specs=[pl.BlockSpec((1, tm, d), lambda i, t: (i, t, 0)),
                  pl.BlockSpec((1, 1, N_MOD, d), lambda i, t: (layer, i, 0, 0)),
                  pl.BlockSpec((1, 4, d), lambda i, t: (layer, 0, 0)),
                  pl.BlockSpec((1, d, 2 * f), lambda i, t: (j, 0, 0)),
                  pl.BlockSpec((1, f, d), lambda i, t: (j, 0, 0))],
        out_specs=pl.BlockSpec((1, tm, d), lambda i, t: (i, t, 0)),
        out_shape=jax.ShapeDtypeStruct(x.shape, F32),
        compiler_params=_params("parallel", "parallel"),
    )(x, mod, norm_g, w_gu, w_down)


def _qkv_kernel(x_ref, mod_ref, ng_ref, w_ref, o_ref):
    x = x_ref[0]
    m = mod_ref[0, 0]
    ng = ng_ref[0]
    d = x.shape[-1]
    h = _norm_mod(x, _row(ng, 0), _row(m, SCALE1), _row(m, SHIFT1)).astype(BF16)
    for c in range(3):
        r = jnp.dot(h, w_ref[0, :, c * d:(c + 1) * d], preferred_element_type=F32)
        if c == 0:
            r = r * (HEAD_DIM ** -0.5)
        o_ref[0, :, c * d:(c + 1) * d] = r.astype(BF16)


def _qkv_layer(x, mod, norm_g, w_qkv, layer, j):
    b, s, d = x.shape
    tm = min(512, s)
    return pl.pallas_call(
        _qkv_kernel,
        grid=(b, s // tm),
        in_specs=[pl.BlockSpec((1, tm, d), lambda i, t: (i, t, 0)),
                  pl.BlockSpec((1, 1, N_MOD, d), lambda i, t: (layer, i, 0, 0)),
                  pl.BlockSpec((1, 4, d), lambda i, t: (layer, 0, 0)),
                  pl.BlockSpec((1, d, 3 * d), lambda i, t: (j, 0, 0))],
        out_specs=pl.BlockSpec((1, tm, 3 * d), lambda i, t: (i, t, 0)),
        out_shape=jax.ShapeDtypeStruct((b, s, 3 * d), BF16),
        compiler_params=_params("parallel", "parallel"),
    )(x, mod, norm_g, w_qkv)


def _bias_kernel(rb_ref, o_ref, *, t):
    h = pl.program_id(0)
    delta = pl.program_id(1)
    r = lax.broadcasted_iota(jnp.int32, (t, t), 0)
    c = lax.broadcasted_iota(jnp.int32, (t, t), 1)
    n_signed = delta * t + r - c
    n = jnp.maximum(n_signed, 0)
    max_exact = N_BUCKETS // 2
    nf = jnp.maximum(n, 1).astype(F32)
    large = max_exact + (jnp.log(nf / max_exact) / math.log(MAX_DISTANCE / max_exact)
                         * (N_BUCKETS - max_exact)).astype(jnp.int32)
    large = jnp.minimum(large, N_BUCKETS - 1)
    bucket = jnp.where(n < max_exact, n, large)
    bias = jnp.zeros((t, t), F32)
    for b in range(N_BUCKETS):
        bias = jnp.where(bucket == b, rb_ref[b * N_HEADS + h], bias)
    o_ref[0, 0] = jnp.where(n_signed < 0, NEG_BIG, bias)


def _bias_tiles(rel_bias, t):
    return pl.pallas_call(
        functools.partial(_bias_kernel, t=t),
        grid=(N_HEADS, 2),
        in_specs=[pl.BlockSpec(memory_space=pltpu.SMEM)],
        out_specs=pl.BlockSpec((1, 1, t, t), lambda h, dlt: (h, dlt, 0, 0)),
        out_shape=jax.ShapeDtypeStruct((N_HEADS, 2, t, t), F32),
        compiler_params=_params("parallel", "parallel"),
    )(rel_bias.reshape(-1))


def _attn_kernel(rb_ref, q_ref, k_ref, v_ref, bias_ref, lv_ref, sg_ref, o_ref,
                 vaug, m_sc, acc_sc, *, t, lam_init):
    h = pl.program_id(1)
    qi = pl.program_id(2)

    @pl.when(qi == 0)
    def _():
        vaug[:, 0:V_HEAD_DIM] = v_ref[0]
        vaug[:, V_HEAD_DIM:] = jnp.ones((vaug.shape[0], V_HEAD_DIM), BF16)

    q = q_ref[0]
    lane = lax.broadcasted_iota(jnp.int32, q.shape, 1)
    zero = jnp.zeros_like(q)
    qs = (jnp.where(lane < HEAD_DIM, q, zero), jnp.where(lane >= HEAD_DIM, q, zero))
    m_sc[...] = jnp.full(m_sc.shape, NEG_BIG, F32)
    acc_sc[...] = jnp.zeros(acc_sc.shape, F32)

    def step(ki, bias):
        ks = pl.multiple_of(ki * t, t)
        kt = k_ref[0, pl.ds(ks, t), :]
        va = vaug[pl.ds(ks, t), :]
        for c in range(2):
            sc = lax.dot_general(qs[c], kt, (((1,), (1,)), ((), ())),
                                 preferred_element_type=F32) + bias
            m_old = m_sc[c]
            m_new = jnp.maximum(m_old, jnp.max(sc, axis=-1, keepdims=True))
            p = jnp.exp(sc - m_new)
            acc_sc[c] = jnp.exp(m_old - m_new) * acc_sc[c] + jnp.dot(
                p.astype(BF16), va, preferred_element_type=F32)
            m_sc[c] = m_new

    far_bias = rb_ref[(N_BUCKETS - 1) * N_HEADS + h]

    def far_body(ki, carry):
        step(ki, far_bias)
        return carry

    lax.fori_loop(0, jnp.maximum(qi - 1, 0), far_body, 0)

    @pl.when(qi >= 1)
    def _():
        step(qi - 1, bias_ref[0, 1])

    step(qi, bias_ref[0, 0])

    lv = lv_ref[0]
    lam = (jnp.exp(jnp.sum(_row(lv, 0) * _row(lv, 1), axis=-1, keepdims=True))
           - jnp.exp(jnp.sum(_row(lv, 2) * _row(lv, 3), axis=-1, keepdims=True)) + lam_init)
    a0 = acc_sc[0]
    a1 = acc_sc[1]
    o = (a0[:, :V_HEAD_DIM] / a0[:, V_HEAD_DIM:]
         - lam * (a1[:, :V_HEAD_DIM] / a1[:, V_HEAD_DIM:]))
    o = _rms(o, sg_ref[0]) * (1.0 - lam_init)
    o_ref[0] = o.astype(BF16)


def _attention(qkv, bias, rel_bias, lambda_vecs, subln_g, j, lam_init, t):
    b, s, d3 = qkv.shape
    d = d3 // 3
    assert t >= MAX_DISTANCE and s % t == 0
    return pl.pallas_call(
        functools.partial(_attn_kernel, t=t, lam_init=lam_init),
        grid=(b, N_HEADS, s // t),
        in_specs=[pl.BlockSpec(memory_space=pltpu.SMEM),
                  pl.BlockSpec((1, t, V_HEAD_DIM), lambda i, h, q: (i, q, h)),
                  pl.BlockSpec((1, s, V_HEAD_DIM), lambda i, h, q: (i, 0, N_HEADS + h)),
                  pl.BlockSpec((1, s, V_HEAD_DIM), lambda i, h, q: (i, 0, 2 * N_HEADS + h)),
                  pl.BlockSpec((1, 2, t, t), lambda i, h, q: (h, 0, 0, 0)),
                  pl.BlockSpec((1, 4, HEAD_DIM), lambda i, h, q: (j, 0, 0)),
                  pl.BlockSpec((1, 1, V_HEAD_DIM), lambda i, h, q: (j, 0, 0))],
        out_specs=pl.BlockSpec((1, t, V_HEAD_DIM), lambda i, h, q: (i, q, h)),
        out_shape=jax.ShapeDtypeStruct((b, s, d), BF16),
        scratch_shapes=[pltpu.VMEM((s, 2 * V_HEAD_DIM), BF16),
                        pltpu.VMEM((2, t, 1), F32),
                        pltpu.VMEM((2, t, 2 * V_HEAD_DIM), F32)],
        compiler_params=_params("arbitrary", "arbitrary", "arbitrary"),
    )(rel_bias.reshape(-1), qkv, qkv, qkv, bias, lambda_vecs,
      subln_g.reshape(-1, 1, V_HEAD_DIM))


def _wo_route_kernel(o_ref, x_ref, mod_ref, ng_ref, wo_ref, wr_ref,
                     xmid_ref, h2_ref, ri_ref, rw_ref, cnt_ref, run_sc, *, tm):
    first = jnp.logical_and(pl.program_id(0) == 0, pl.program_id(1) == 0)

    @pl.when(first)
    def _():
        run_sc[...] = jnp.zeros(run_sc.shape, F32)

    x = x_ref[0]
    m = mod_ref[0, 0]
    ng = ng_ref[0]
    y = jnp.dot(o_ref[0], wo_ref[0], preferred_element_type=F32)
    xn = x + _row(m, GATE1) * _rms(y, _row(ng, 1))
    xmid_ref[0] = xn
    h2 = _norm_mod(xn, _row(ng, 2), _row(m, SCALE2), _row(m, SHIFT2))
    h2_ref[0] = h2

    logits = jnp.dot(h2, wr_ref[0], precision=lax.Precision.HIGHEST, preferred_element_type=F32)
    lane = lax.broadcasted_iota(jnp.int32, logits.shape, 1)
    lane_f = lane.astype(F32)
    neg_inf = jnp.float32(-jnp.inf)
    logits = jnp.where(lane < N_EXPERTS, logits, neg_inf)
    m1 = jnp.max(logits, axis=-1, keepdims=True)
    i1 = jnp.min(jnp.where(logits == m1, lane_f, float(LANES)), axis=-1, keepdims=True)
    oh1 = lane_f == i1
    rest = jnp.where(oh1, neg_inf, logits)
    m2 = jnp.max(rest, axis=-1, keepdims=True)
    i2 = jnp.min(jnp.where(rest == m2, lane_f, float(LANES)), axis=-1, keepdims=True)
    oh2 = lane_f == i2
    i1 = i1.astype(jnp.int32)
    i2 = i2.astype(jnp.int32)
    e21 = jnp.exp(m2 - m1)
    w1 = 1.0 / (1.0 + e21)
    w2 = e21 * w1

    oh = jnp.where(jnp.logical_or(oh1, oh2), 1.0, 0.0)
    r = lax.broadcasted_iota(jnp.int32, (tm, tm), 0)
    c = lax.broadcasted_iota(jnp.int32, (tm, tm), 1)
    tri = jnp.where(r > c, 1.0, 0.0).astype(BF16)
    prefix = jnp.dot(tri, oh.astype(BF16), preferred_element_type=F32) + run_sc[...]
    r1 = jnp.sum(jnp.where(oh1, prefix, 0.0), axis=-1, keepdims=True).astype(jnp.int32)
    r2 = jnp.sum(jnp.where(oh2, prefix, 0.0), axis=-1, keepdims=True).astype(jnp.int32)
    run_sc[...] = run_sc[...] + jnp.sum(oh, axis=0, keepdims=True)

    ri_ref[0] = jnp.where(lane == 0, i1, jnp.where(lane == 1, i2,
                          jnp.where(lane == 2, r1, jnp.where(lane == 3, r2, 0))))
    rw_ref[0] = jnp.where(lane == 0, w1, jnp.where(lane == 1, w2, 0.0))
    cnt_ref[...] = run_sc[...]


def _wo_route_layer(o, x, mod, norm_g, w_o, w_router, layer, j):
    b, s, d = x.shape
    tm = min(512, s)
    act = jax.ShapeDtypeStruct((b, s, d), F32)
    tile = pl.BlockSpec((1, tm, d), lambda i, t: (i, t, 0))
    lanes = pl.BlockSpec((1, tm, LANES), lambda i, t: (i, t, 0))
    return pl.pallas_call(
        functools.partial(_wo_route_kernel, tm=tm),
        grid=(b, s // tm),
        in_specs=[tile, tile,
                  pl.BlockSpec((1, 1, N_MOD, d), lambda i, t: (layer, i, 0, 0)),
                  pl.BlockSpec((1, 4, d), lambda i, t: (layer, 0, 0)),
                  pl.BlockSpec((1, d, d), lambda i, t: (j, 0, 0)),
                  pl.BlockSpec((1, d, LANES), lambda i, t: (j, 0, 0))],
        out_specs=[tile, tile, lanes, lanes, pl.BlockSpec((1, LANES), lambda i, t: (0, 0))],
        out_shape=[act, act,
                   jax.ShapeDtypeStruct((b, s, LANES), jnp.int32),
                   jax.ShapeDtypeStruct((b, s, LANES), F32),
                   jax.ShapeDtypeStruct((1, LANES), F32)],
        scratch_shapes=[pltpu.VMEM((1, LANES), F32)],
        compiler_params=_params("arbitrary", "arbitrary"),
    )(o, x, mod, norm_g, w_o, w_router)


def _row_copy(src, src_row, dst, dst_row, sem):
    return pltpu.make_async_copy(src.at[pl.ds(src_row, 1)], dst.at[pl.ds(dst_row, 1)], sem)


def _scatter_kernel(offs_ref, ri_ref, h_ref, xs_ref, zbuf, sem, *, tm, tg):
    @pl.when(pl.program_id(0) == 0)
    def _():
        zbuf[...] = jnp.zeros(zbuf.shape, F32)
        for e in range(N_EXPERTS):
            @pl.when(offs_ref[e + 1] > offs_ref[e])
            def _():
                start = pl.multiple_of(offs_ref[e + 1] - tg, tg)
                cp = pltpu.make_async_copy(zbuf, xs_ref.at[pl.ds(start, tg)], sem)
                cp.start()
                cp.wait()

    def issue(t, carry):
        for k in range(2):
            dst = offs_ref[ri_ref[4 * t + k]] + ri_ref[4 * t + 2 + k]
            _row_copy(h_ref, t, xs_ref, dst, sem).start()
        return carry

    lax.fori_loop(0, tm, issue, 0)

    def drain(t, carry):
        for k in range(2):
            _row_copy(h_ref, 0, xs_ref, 0, sem).wait()
        return carry

    lax.fori_loop(0, tm, drain, 0)


def _scatter_rows(offs, ri_flat, h2, rows, tg):
    n, d = h2.shape
    tm = min(256, n)
    return pl.pallas_call(
        functools.partial(_scatter_kernel, tm=tm, tg=tg),
        grid_spec=pltpu.PrefetchScalarGridSpec(
            num_scalar_prefetch=1,
            grid=(n // tm,),
            in_specs=[pl.BlockSpec((4 * tm,), lambda i, offs: (i,), memory_space=pltpu.SMEM),
                      pl.BlockSpec((tm, d), lambda i, offs: (i, 0))],
            out_specs=pl.BlockSpec(memory_space=pl.ANY),
            scratch_shapes=[pltpu.VMEM((tg, d), F32), pltpu.SemaphoreType.DMA(())]),
        out_shape=jax.ShapeDtypeStruct((rows, d), F32),
        compiler_params=_params("arbitrary"),
    )(offs, ri_flat, h2)


def _gmm_kernel(te_ref, nu_ref, x_ref, wg_ref, wu_ref, wd_ref, o_ref, xb, acc):
    i = pl.program_id(0)
    jf = pl.program_id(1)

    @pl.when(i < nu_ref[0])
    def _():
        @pl.when(jf == 0)
        def _():
            xb[...] = x_ref[...].astype(BF16)
            acc[...] = jnp.zeros(acc.shape, F32)

        x = xb[...]
        gate = jnp.dot(x, wg_ref[...], preferred_element_type=F32)
        up = jnp.dot(x, wu_ref[...], preferred_element_type=F32)
        hm = (_silu(gate) * up).astype(BF16)
        acc[...] += jnp.dot(hm, wd_ref[...], preferred_element_type=F32)

        @pl.when(jf == pl.num_programs(1) - 1)
        def _():
            o_ref[...] = acc[...]


def _expert_ffn(tile_expert, n_used, xs, w_gu, w_down, tg, j):
    rows, d = xs.shape
    f = w_down.shape[2]
    fc = 512 if f % 512 == 0 else f
    nf = f // fc

    def row_map(i, jf, te, nu):
        return (jnp.minimum(i, nu[0] - 1), 0)

    def w_map(col0):
        def index_map(i, jf, te, nu):
            last = nu[0] - 1
            return (j, te[jnp.minimum(i, last)], 0, col0 + jnp.where(i <= last, jf, nf - 1))
        return index_map

    def wd_map(i, jf, te, nu):
        last = nu[0] - 1
        return (j, te[jnp.minimum(i, last)], jnp.where(i <= last, jf, nf - 1), 0)

    return pl.pallas_call(
        _gmm_kernel,
        grid_spec=pltpu.PrefetchScalarGridSpec(
            num_scalar_prefetch=2,
            grid=(rows // tg, nf),
            in_specs=[pl.BlockSpec((tg, d), row_map),
                      pl.BlockSpec((None, None, d, fc), w_map(0)),
                      pl.BlockSpec((None, None, d, fc), w_map(nf)),
                      pl.BlockSpec((None, None, fc, d), wd_map)],
            out_specs=pl.BlockSpec((tg, d), row_map),
            scratch_shapes=[pltpu.VMEM((tg, d), BF16), pltpu.VMEM((tg, d), F32)]),
        out_shape=jax.ShapeDtypeStruct((rows, d), F32),
        compiler_params=_params("arbitrary", "arbitrary"),
    )(tile_expert, n_used, xs, w_gu, w_gu, w_down)


def _combine_kernel(offs_ref, ri_ref, rw_ref, x_ref, mod_ref, ng_ref, ys_ref, o_ref,
                    ybuf, sem, *, tm):
    def issue(t, carry):
        for k in range(2):
            src = offs_ref[ri_ref[4 * t + k]] + ri_ref[4 * t + 2 + k]
            _row_copy(ys_ref, src, ybuf.at[k], t, sem).start()
        return carry

    lax.fori_loop(0, tm, issue, 0)

    def drain(t, carry):
        for k in range(2):
            _row_copy(ys_ref, 0, ybuf.at[k], 0, sem).wait()
        return carry

    lax.fori_loop(0, tm, drain, 0)

    m = mod_ref[0, 0]
    ng = ng_ref[0]
    rw = rw_ref[0]
    y = rw[:, 0:1] * ybuf[0] + rw[:, 1:2] * ybuf[1]
    o_ref[0] = x_ref[0] + _row(m, GATE2) * _rms(y, _row(ng, 3))


def _combine_layer(offs, ri_flat, rw, x, mod, norm_g, ys, layer):
    b, s, d = x.shape
    tm = min(256, s)
    nt = s // tm
    return pl.pallas_call(
        functools.partial(_combine_kernel, tm=tm),
        grid_spec=pltpu.PrefetchScalarGridSpec(
            num_scalar_prefetch=1,
            grid=(b, nt),
            in_specs=[pl.BlockSpec((4 * tm,), lambda i, t, offs: (i * nt + t,),
                                   memory_space=pltpu.SMEM),
                      pl.BlockSpec((1, tm, LANES), lambda i, t, offs: (i, t, 0)),
                      pl.BlockSpec((1, tm, d), lambda i, t, offs: (i, t, 0)),
                      pl.BlockSpec((1, 1, N_MOD, d), lambda i, t, offs: (layer, i, 0, 0)),
                      pl.BlockSpec((1, 4, d), lambda i, t, offs: (layer, 0, 0)),
                      pl.BlockSpec(memory_space=pl.ANY)],
            out_specs=pl.BlockSpec((1, tm, d), lambda i, t, offs: (i, t, 0)),
            scratch_shapes=[pltpu.VMEM((2, tm, d), F32), pltpu.SemaphoreType.DMA(())]),
        out_shape=jax.ShapeDtypeStruct(x.shape, F32),
        compiler_params=_params("arbitrary", "arbitrary"),
    )(offs, ri_flat, rw, x, mod, norm_g, ys)


def _moe_layer(h2, ri, rw, counts, xmid, mod, norm_g, w_gu, w_down, layer, j):
    b, s, d = xmid.shape
    n = b * s
    tg = min(1024, n)
    cnt = counts[0, :N_EXPERTS].astype(jnp.int32)
    padded = (cnt + tg - 1) // tg * tg
    offs = jnp.concatenate([jnp.zeros((1,), jnp.int32), jnp.cumsum(padded)]).astype(jnp.int32)
    n_tiles = 2 * n // tg + N_EXPERTS
    starts = jnp.arange(n_tiles, dtype=jnp.int32) * tg
    tile_expert = jnp.minimum(jnp.sum(starts[:, None] >= offs[None, 1:], axis=1),
                              N_EXPERTS - 1).astype(jnp.int32)
    n_used = (offs[N_EXPERTS:] // tg).astype(jnp.int32)
    ri_flat = ri[:, :, :4].reshape(-1)

    xs = _scatter_rows(offs, ri_flat, h2.reshape(n, d), n_tiles * tg, tg)
    ys = _expert_ffn(tile_expert, n_used, xs, w_gu, w_down, tg, j)
    return _combine_layer(offs, ri_flat, rw, xmid, mod, norm_g, ys, layer)


def kernel(x, c, w_mod, b_mod, norm_g, pool_w, pool_scale, w_qkv, w_o, subln_g, lambda_vecs,
           rel_bias, ffn_w_gu, ffn_w_down, w_router, moe_w_gu, moe_w_down):
    depth = w_mod.shape[0]
    s = x.shape[1]
    d = x.shape[2]
    assert d == 2 * N_HEADS * HEAD_DIM
    t_attn = min(256, s)

    mod = _modulation(c, w_mod, b_mod)
    pool_w16 = pool_w.astype(BF16)
    ffn_gu16 = ffn_w_gu.astype(BF16)
    ffn_down16 = ffn_w_down.astype(BF16)
    qkv16 = w_qkv.astype(BF16)
    wo16 = w_o.astype(BF16)
    moe_gu16 = moe_w_gu.astype(BF16)
    moe_down16 = moe_w_down.astype(BF16)
    router_pad = jnp.pad(w_router, ((0, 0), (0, 0), (0, LANES - N_EXPERTS)))
    bias = _bias_tiles(rel_bias, t_attn)

    for i in range(depth):
        j = i // 2
        if i % 2 == 0:
            x = _pool_layer(x, mod, norm_g, pool_w16, pool_scale, i, j)
            x = _ffn_layer(x, mod, norm_g, ffn_gu16, ffn_down16, i, j)
        else:
            lam_init = 0.8 - 0.6 * math.exp(-0.3 * i)
            qkv = _qkv_layer(x, mod, norm_g, qkv16, i, j)
            o = _attention(qkv, bias, rel_bias, lambda_vecs, subln_g, j, lam_init, t_attn)
            xmid, h2, ri, rw, counts = _wo_route_layer(o, x, mod, norm_g, wo16, router_pad, i, j)
            x = _moe_layer(h2, ri, rw, counts, xmid, mod, norm_g, moe_gu16, moe_down16, i, j)
    return x
```

```python
import functools
import math

import jax
import jax.numpy as jnp
from jax import lax
from jax.experimental import pallas as pl
from jax.experimental.pallas import tpu as pltpu

F32 = jnp.float32
BF16 = jnp.bfloat16

EPS = 1e-6
N_MOD = 6
POOL_WINDOWS = (2, 4, 8, 16)
POOL_HALO = 16
N_HEADS = 8
HEAD_DIM = 64
V_HEAD_DIM = 2 * HEAD_DIM
N_BUCKETS = 32
MAX_DISTANCE = 128
N_EXPERTS = 8
LANES = 128
NEG_BIG = -1e30
MASKED_TILE = 2 * NEG_BIG
TILE_DIAG, TILE_SUB, TILE_FAR, TILE_MASKED = range(4)
N_TILE_KINDS = 4
ONES_ROWS = 16
VMEM_LIMIT = 56 * 1024 * 1024

SHIFT1, SCALE1, GATE1, SHIFT2, SCALE2, GATE2 = range(N_MOD)


def _params(*sem):
    return pltpu.CompilerParams(dimension_semantics=sem, vmem_limit_bytes=VMEM_LIMIT)


def _rms(x, g):
    return x * lax.rsqrt(jnp.mean(x * x, axis=-1, keepdims=True) + EPS) * g


def _norm_mod(x, g, scale, shift):
    return _rms(x, g) * (1.0 + scale) + shift


def _silu(x):
    return x * (1.0 / (1.0 + jnp.exp(-x)))


def _row(a, i):
    return a[i:i + 1, :]


def _mod_kernel(c_ref, w_ref, b_ref, o_ref):
    c = _silu(c_ref[...])
    o_ref[0] = jnp.dot(c, w_ref[0], precision=lax.Precision.HIGHEST,
                       preferred_element_type=F32) + b_ref[0]


def _modulation(c, w_mod, b_mod):
    depth, d, e = w_mod.shape
    b = c.shape[0]
    nc = 1536
    out = pl.pallas_call(
        _mod_kernel,
        name="adaln_mod",
        grid=(depth, e // nc),
        in_specs=[pl.BlockSpec((b, d), lambda l, j: (0, 0)),
                  pl.BlockSpec((1, d, nc), lambda l, j: (l, 0, j)),
                  pl.BlockSpec((1, 1, nc), lambda l, j: (l, 0, j))],
        out_specs=pl.BlockSpec((1, b, nc), lambda l, j: (l, 0, j)),
        out_shape=jax.ShapeDtypeStruct((depth, b, e), F32),
        compiler_params=_params("parallel", "parallel"),
    )(c, w_mod, b_mod.reshape(depth, 1, e))
    return out.reshape(depth, b, N_MOD, d)


def _pool_kernel(x_ref, mod_ref, ng_ref, pw_ref, ps_ref, o_ref, hbuf, *, ts):
    s = pl.program_id(1)
    x = x_ref[0]
    m = mod_ref[0, 0]
    ng = ng_ref[0]
    d = x.shape[-1]
    cg = d // len(POOL_WINDOWS)
    h = _norm_mod(x, _row(ng, 0), _row(m, SCALE1), _row(m, SHIFT1))

    @pl.when(s == 0)
    def _():
        hbuf[0:POOL_HALO, :] = jnp.zeros((POOL_HALO, d), F32)

    @pl.when(s > 0)
    def _():
        hbuf[0:POOL_HALO, :] = hbuf[ts:ts + POOL_HALO, :]

    hbuf[POOL_HALO:POOL_HALO + ts, :] = h
    count = s * ts + lax.broadcasted_iota(jnp.int32, (ts, 1), 0) + 1
    outs = []
    for g, w in enumerate(POOL_WINDOWS):
        lo = g * cg
        hg = h[:, lo:lo + cg]
        acc = hg
        for k in range(1, w):
            acc = acc + hbuf[POOL_HALO - k:POOL_HALO - k + ts, lo:lo + cg]
        pooled = acc / jnp.minimum(count, w).astype(F32) - hg
        outs.append(jnp.dot(pooled.astype(BF16), pw_ref[0, g], preferred_element_type=F32))
    y = jnp.concatenate(outs, axis=-1) * ps_ref[0]
    o_ref[0] = x + _row(m, GATE1) * _rms(y, _row(ng, 1))


def _pool_layer(x, mod, norm_g, pool_w, pool_scale, layer, j):
    b, s, d = x.shape
    ts = min(512, s)
    g, cg, _ = pool_w.shape[1:]
    return pl.pallas_call(
        functools.partial(_pool_kernel, ts=ts),
        name="pool_mixer",
        grid=(b, s // ts),
        in_specs=[pl.BlockSpec((1, ts, d), lambda i, t: (i, t, 0)),
                  pl.BlockSpec((1, 1, N_MOD, d), lambda i, t: (layer, i, 0, 0)),
                  pl.BlockSpec((1, 4, d), lambda i, t: (layer, 0, 0)),
                  pl.BlockSpec((1, g, cg, cg), lambda i, t: (j, 0, 0, 0)),
                  pl.BlockSpec((1, 1, d), lambda i, t: (j, 0, 0))],
        out_specs=pl.BlockSpec((1, ts, d), lambda i, t: (i, t, 0)),
        out_shape=jax.ShapeDtypeStruct(x.shape, F32),
        scratch_shapes=[pltpu.VMEM((POOL_HALO + ts, d), F32)],
        compiler_params=_params("arbitrary", "arbitrary"),
    )(x, mod, norm_g, pool_w, pool_scale.reshape(-1, 1, d))


def _ffn_kernel(x_ref, mod_ref, ng_ref, wgu_ref, wd_ref, o_ref, *, f, fc):
    x = x_ref[0]
    m = mod_ref[0, 0]
    ng = ng_ref[0]
    h = _norm_mod(x, _row(ng, 2), _row(m, SCALE2), _row(m, SHIFT2)).astype(BF16)
    acc = jnp.zeros(x.shape, F32)
    for c in range(f // fc):
        gate = jnp.dot(h, wgu_ref[0, :, c * fc:(c + 1) * fc], preferred_element_type=F32)
        up = jnp.dot(h, wgu_ref[0, :, f + c * fc:f + (c + 1) * fc], preferred_element_type=F32)
        hm = (_silu(gate) * up).astype(BF16)
        acc = acc + jnp.dot(hm, wd_ref[0, c * fc:(c + 1) * fc, :], preferred_element_type=F32)
    o_ref[0] = x + _row(m, GATE2) * _rms(acc, _row(ng, 3))


def _ffn_layer(x, mod, norm_g, w_gu, w_down, layer, j):
    b, s, d = x.shape
    f = w_down.shape[1]
    tm = min(512, s)
    fc = 256 if f % 256 == 0 else f
    return pl.pallas_call(
        functools.partial(_ffn_kernel, f=f, fc=fc),
        name="dense_swiglu",
        grid=(b, s // tm),
        in_specs=[pl.BlockSpec((1, tm, d), lambda i, t: (i, t, 0)),
                  pl.BlockSpec((1, 1, N_MOD, d), lambda i, t: (layer, i, 0, 0)),
                  pl.BlockSpec((1, 4, d), lambda i, t: (layer, 0, 0)),
                  pl.BlockSpec((1, d, 2 * f), lambda i, t: (j, 0, 0)),
                  pl.BlockSpec((1, f, d), lambda i, t: (j, 0, 0))],
        out_specs=pl.BlockSpec((1, tm, d), lambda i, t: (i, t, 0)),
        out_shape=jax.ShapeDtypeStruct(x.shape, F32),
        compiler_params=_params("parallel", "parallel"),
    )(x, mod, norm_g, w_gu, w_down)


def _qkv_kernel(x_ref, mod_ref, ng_ref, wqk_ref, wvt_ref, qk_ref, vt_ref, *, t):
    x = x_ref[0]
    m = mod_ref[0, 0]
    ng = ng_ref[0]
    d = x.shape[-1]
    h = _norm_mod(x, _row(ng, 0), _row(m, SCALE1), _row(m, SHIFT1)).astype(BF16)
    for c in range(2):
        r = jnp.dot(h, wqk_ref[0, :, c * d:(c + 1) * d], preferred_element_type=F32)
        if c == 0:
            r = r * (HEAD_DIM ** -0.5)
        qk_ref[0, :, c * d:(c + 1) * d] = r.astype(BF16)
    vt = lax.dot_general(wvt_ref[0], h, (((1,), (1,)), ((), ())), preferred_element_type=F32)
    for kk in range(x.shape[0] // t):
        vt_ref[0, kk] = vt[:, kk * t:(kk + 1) * t].astype(BF16)


def _qkv_layer(x, mod, norm_g, w_qk, w_vt, layer, j, t):
    b, s, d = x.shape
    tm = min(512, s)
    return pl.pallas_call(
        functools.partial(_qkv_kernel, t=t),
        name="qkv_proj",
        grid=(b, s // tm),
        in_specs=[pl.BlockSpec((1, tm, d), lambda i, u: (i, u, 0)),
                  pl.BlockSpec((1, 1, N_MOD, d), lambda i, u: (layer, i, 0, 0)),
                  pl.BlockSpec((1, 4, d), lambda i, u: (layer, 0, 0)),
                  pl.BlockSpec((1, d, 2 * d), lambda i, u: (j, 0, 0)),
                  pl.BlockSpec((1, d, d), lambda i, u: (j, 0, 0))],
        out_specs=[pl.BlockSpec((1, tm, 2 * d), lambda i, u: (i, u, 0)),
                   pl.BlockSpec((1, tm // t, d, t), lambda i, u: (i, u, 0, 0))],
        out_shape=[jax.ShapeDtypeStruct((b, s, 2 * d), BF16),
                   jax.ShapeDtypeStruct((b, s // t, d, t), BF16)],
        compiler_params=_params("parallel", "parallel"),
    )(x, mod, norm_g, w_qk, w_vt)


def _bias_kernel(rb_ref, o_ref, *, t):
    h = pl.program_id(0)
    kind = pl.program_id(1)
    key = lax.broadcasted_iota(jnp.int32, (t, t), 0)
    qry = lax.broadcasted_iota(jnp.int32, (t, t), 1)
    n_signed = kind * t + qry - key
    n = jnp.maximum(n_signed, 0)
    max_exact = N_BUCKETS // 2
    nf = jnp.maximum(n, 1).astype(F32)
    large = max_exact + (jnp.log(nf / max_exact) / math.log(MAX_DISTANCE / max_exact)
                         * (N_BUCKETS - max_exact)).astype(jnp.int32)
    large = jnp.minimum(large, N_BUCKETS - 1)
    bucket = jnp.where(n < max_exact, n, large)
    bias = jnp.zeros((t, t), F32)
    for b in range(N_BUCKETS):
        bias = jnp.where(bucket == b, rb_ref[b * N_HEADS + h], bias)
    bias = jnp.where(n_signed < 0, NEG_BIG, bias)
    o_ref[0, 0] = jnp.where(kind == TILE_MASKED, MASKED_TILE, bias)


def _bias_tiles(rel_bias, t):
    return pl.pallas_call(
        functools.partial(_bias_kernel, t=t),
        name="t5_bias_tiles",
        grid=(N_HEADS, N_TILE_KINDS),
        in_specs=[pl.BlockSpec(memory_space=pltpu.SMEM)],
        out_specs=pl.BlockSpec((1, 1, t, t), lambda h, kind: (h, kind, 0, 0)),
        out_shape=jax.ShapeDtypeStruct((N_HEADS, N_TILE_KINDS, t, t), F32),
        compiler_params=_params("parallel", "parallel"),
    )(rel_bias.reshape(-1))


def _attn_kernel(q_ref, k_ref, vt_ref, bias_ref, lv_ref, sg_ref, o_ref,
                 s0, s1, p0, p1, al0, al1, m_sc, acc_sc, *, t, lam_init):
    qi = pl.program_id(2)
    n_tiles = qi + 1

    q = q_ref[0]
    lane = lax.broadcasted_iota(jnp.int32, q.shape, 1)
    zero = jnp.zeros_like(q)
    qs = (jnp.where(lane < HEAD_DIM, q, zero), jnp.where(lane >= HEAD_DIM, q, zero))
    m_sc[...] = jnp.full(m_sc.shape, NEG_BIG, F32)
    acc_sc[...] = jnp.zeros(acc_sc.shape, F32)
    s1[...] = jnp.full(s1.shape, MASKED_TILE, F32)
    p0[...] = jnp.zeros(p0.shape, BF16)
    al0[...] = jnp.ones(al0.shape, F32)
    ones = jnp.ones((ONES_ROWS, t), BF16)

    def scores(tile, s_out):
        tl = jnp.minimum(tile, qi)
        kind = jnp.where(tile > qi, TILE_MASKED, jnp.minimum(qi - tl, TILE_FAR))
        kt = k_ref[0, pl.ds(pl.multiple_of(tl * t, t), t), :]
        bias = bias_ref[0, kind]
        for c in range(2):
            s_out[c] = lax.dot_general(kt, qs[c], (((1,), (1,)), ((), ())),
                                       preferred_element_type=F32) + bias

    def softmax(s_in, p_out, al_out):
        for c in range(2):
            sc = s_in[c]
            m_old = m_sc[c]
            m_new = jnp.maximum(m_old, jnp.max(sc, axis=0, keepdims=True))
            p_out[c] = jnp.exp(sc - m_new).astype(BF16)
            al_out[c] = jnp.exp(m_old - m_new)
            m_sc[c] = m_new

    def values(tile, p_in, al_in):
        tl = jnp.clip(tile, 0, qi)
        va = jnp.concatenate([vt_ref[0, tl], ones], axis=0)
        for c in range(2):
            acc_sc[c] = al_in[c] * acc_sc[c] + jnp.dot(va, p_in[c], preferred_element_type=F32)

    def tick_pair(jp, carry):
        it = 2 * jp
        scores(it, s0)
        softmax(s1, p1, al1)
        values(it - 2, p0, al0)
        scores(it + 1, s1)
        softmax(s0, p0, al0)
        values(it - 1, p1, al1)
        return carry

    lax.fori_loop(0, lax.shift_right_logical(n_tiles + 3, 1), tick_pair, 0)

    lv = lv_ref[0]
    lam = (jnp.exp(jnp.sum(_row(lv, 0) * _row(lv, 1), axis=-1, keepdims=True))
           - jnp.exp(jnp.sum(_row(lv, 2) * _row(lv, 3), axis=-1, keepdims=True)) + lam_init)
    a0 = acc_sc[0]
    a1 = acc_sc[1]
    ot = (a0[:V_HEAD_DIM] / a0[V_HEAD_DIM:V_HEAD_DIM + 1]
          - lam * (a1[:V_HEAD_DIM] / a1[V_HEAD_DIM:V_HEAD_DIM + 1]))
    ot = ot * lax.rsqrt(jnp.mean(ot * ot, axis=0, keepdims=True) + EPS)
    o = ot.T * sg_ref[0] * (1.0 - lam_init)
    o_ref[0] = o.astype(BF16)


def _attention(qk, vt, bias, lambda_vecs, subln_g, j, lam_init, t):
    b, s, d2 = qk.shape
    d = d2 // 2
    assert t >= MAX_DISTANCE and s % t == 0
    score_buf = pltpu.VMEM((2, t, t), F32)
    prob_buf = pltpu.VMEM((2, t, t), BF16)
    alpha_buf = pltpu.VMEM((2, 1, t), F32)
    return pl.pallas_call(
        functools.partial(_attn_kernel, t=t, lam_init=lam_init),
        name="diff_attention",
        grid=(b, N_HEADS, s // t),
        in_specs=[pl.BlockSpec((1, t, V_HEAD_DIM), lambda i, h, q: (i, q, h)),
                  pl.BlockSpec((1, s, V_HEAD_DIM), lambda i, h, q: (i, 0, N_HEADS + h)),
                  pl.BlockSpec((1, s // t, V_HEAD_DIM, t), lambda i, h, q: (i, 0, h, 0)),
                  pl.BlockSpec((1, N_TILE_KINDS, t, t), lambda i, h, q: (h, 0, 0, 0)),
                  pl.BlockSpec((1, 4, HEAD_DIM), lambda i, h, q: (j, 0, 0)),
                  pl.BlockSpec((1, 1, V_HEAD_DIM), lambda i, h, q: (j, 0, 0))],
        out_specs=pl.BlockSpec((1, t, V_HEAD_DIM), lambda i, h, q: (i, q, h)),
        out_shape=jax.ShapeDtypeStruct((b, s, d), BF16),
        scratch_shapes=[score_buf, score_buf, prob_buf, prob_buf, alpha_buf, alpha_buf,
                        pltpu.VMEM((2, 1, t), F32),
                        pltpu.VMEM((2, V_HEAD_DIM + ONES_ROWS, t), F32)],
        compiler_params=_params("arbitrary", "arbitrary", "arbitrary"),
    )(qk, qk, vt, bias, lambda_vecs, subln_g.reshape(-1, 1, V_HEAD_DIM))


def _wo_route_kernel(o_ref, x_ref, mod_ref, ng_ref, wo_ref, wr_ref,
                     xmid_ref, h2_ref, ri_ref, rw_ref, cnt_ref, run_sc, *, tm):
    first = jnp.logical_and(pl.program_id(0) == 0, pl.program_id(1) == 0)

    @pl.when(first)
    def _():
        run_sc[...] = jnp.zeros(run_sc.shape, F32)

    x = x_ref[0]
    m = mod_ref[0, 0]
    ng = ng_ref[0]
    y = jnp.dot(o_ref[0], wo_ref[0], preferred_element_type=F32)
    xn = x + _row(m, GATE1) * _rms(y, _row(ng, 1))
    xmid_ref[0] = xn
    h2 = _norm_mod(xn, _row(ng, 2), _row(m, SCALE2), _row(m, SHIFT2))
    h2_ref[0] = h2

    logits = jnp.dot(h2, wr_ref[0], precision=lax.Precision.HIGHEST, preferred_element_type=F32)
    lane = lax.broadcasted_iota(jnp.int32, logits.shape, 1)
    lane_f = lane.astype(F32)
    neg_inf = jnp.float32(-jnp.inf)
    logits = jnp.where(lane < N_EXPERTS, logits, neg_inf)
    m1 = jnp.max(logits, axis=-1, keepdims=True)
    i1 = jnp.min(jnp.where(logits == m1, lane_f, float(LANES)), axis=-1, keepdims=True)
    oh1 = lane_f == i1
    rest = jnp.where(oh1, neg_inf, logits)
    m2 = jnp.max(rest, axis=-1, keepdims=True)
    i2 = jnp.min(jnp.where(rest == m2, lane_f, float(LANES)), axis=-1, keepdims=True)
    oh2 = lane_f == i2
    i1 = i1.astype(jnp.int32)
    i2 = i2.astype(jnp.int32)
    e21 = jnp.exp(m2 - m1)
    w1 = 1.0 / (1.0 + e21)
    w2 = e21 * w1

    oh = jnp.where(jnp.logical_or(oh1, oh2), 1.0, 0.0)
    r = lax.broadcasted_iota(jnp.int32, (tm, tm), 0)
    c = lax.broadcasted_iota(jnp.int32, (tm, tm), 1)
    tri = jnp.where(r > c, 1.0, 0.0).astype(BF16)
    prefix = jnp.dot(tri, oh.astype(BF16), preferred_element_type=F32) + run_sc[...]
    r1 = jnp.sum(jnp.where(oh1, prefix, 0.0), axis=-1, keepdims=True).astype(jnp.int32)
    r2 = jnp.sum(jnp.where(oh2, prefix, 0.0), axis=-1, keepdims=True).astype(jnp.int32)
    run_sc[...] = run_sc[...] + jnp.sum(oh, axis=0, keepdims=True)

    ri_ref[0] = jnp.where(lane == 0, i1, jnp.where(lane == 1, i2,
                          jnp.where(lane == 2, r1, jnp.where(lane == 3, r2, 0))))
    rw_ref[0] = jnp.where(lane == 0, w1, jnp.where(lane == 1, w2, 0.0))
    cnt_ref[...] = run_sc[...]


def _wo_route_layer(o, x, mod, norm_g, w_o, w_router, layer, j):
    b, s, d = x.shape
    tm = min(512, s)
    act = jax.ShapeDtypeStruct((b, s, d), F32)
    tile = pl.BlockSpec((1, tm, d), lambda i, t: (i, t, 0))
    lanes = pl.BlockSpec((1, tm, LANES), lambda i, t: (i, t, 0))
    return pl.pallas_call(
        functools.partial(_wo_route_kernel, tm=tm),
        name="attn_out_router",
        grid=(b, s // tm),
        in_specs=[tile, tile,
                  pl.BlockSpec((1, 1, N_MOD, d), lambda i, t: (layer, i, 0, 0)),
                  pl.BlockSpec((1, 4, d), lambda i, t: (layer, 0, 0)),
                  pl.BlockSpec((1, d, d), lambda i, t: (j, 0, 0)),
                  pl.BlockSpec((1, d, LANES), lambda i, t: (j, 0, 0))],
        out_specs=[tile, tile, lanes, lanes, pl.BlockSpec((1, LANES), lambda i, t: (0, 0))],
        out_shape=[act, act,
                   jax.ShapeDtypeStruct((b, s, LANES), jnp.int32),
                   jax.ShapeDtypeStruct((b, s, LANES), F32),
                   jax.ShapeDtypeStruct((1, LANES), F32)],
        scratch_shapes=[pltpu.VMEM((1, LANES), F32)],
        compiler_params=_params("arbitrary", "arbitrary"),
    )(o, x, mod, norm_g, w_o, w_router)


def _row_copy(src, src_row, dst, dst_row, sem):
    return pltpu.make_async_copy(src.at[pl.ds(src_row, 1)], dst.at[pl.ds(dst_row, 1)], sem)


ROW_UNROLL = 8


def _scatter_kernel(offs_ref, dst_ref, h_ref, xs_ref, zbuf, sem, *, tm, tg):
    @pl.when(pl.program_id(0) == 0)
    def _():
        zbuf[...] = jnp.zeros(zbuf.shape, F32)
        for e in range(N_EXPERTS):
            @pl.when(offs_ref[e + 1] > offs_ref[e])
            def _():
                start = pl.multiple_of(offs_ref[e + 1] - tg, tg)
                cp = pltpu.make_async_copy(zbuf, xs_ref.at[pl.ds(start, tg)], sem)
                cp.start()
                cp.wait()

    def issue(g, carry):
        for u in range(ROW_UNROLL):
            t = g * ROW_UNROLL + u
            for k in range(2):
                _row_copy(h_ref, t, xs_ref, dst_ref[2 * t + k], sem).start()
        return carry

    lax.fori_loop(0, tm // ROW_UNROLL, issue, 0)

    def drain(g, carry):
        for _ in range(2 * ROW_UNROLL):
            _row_copy(h_ref, 0, xs_ref, 0, sem).wait()
        return carry

    lax.fori_loop(0, tm // ROW_UNROLL, drain, 0)


def _scatter_rows(offs, dest_flat, h2, rows, tg):
    n, d = h2.shape
    tm = min(512, n)
    return pl.pallas_call(
        functools.partial(_scatter_kernel, tm=tm, tg=tg),
        name="moe_group_rows",
        grid_spec=pltpu.PrefetchScalarGridSpec(
            num_scalar_prefetch=1,
            grid=(n // tm,),
            in_specs=[pl.BlockSpec((2 * tm,), lambda i, offs: (i,), memory_space=pltpu.SMEM),
                      pl.BlockSpec((tm, d), lambda i, offs: (i, 0))],
            out_specs=pl.BlockSpec(memory_space=pl.ANY),
            scratch_shapes=[pltpu.VMEM((tg, d), F32), pltpu.SemaphoreType.DMA(())]),
        out_shape=jax.ShapeDtypeStruct((rows, d), F32),
        compiler_params=_params("arbitrary"),
    )(offs, dest_flat, h2)


def _gmm_kernel(te_ref, nu_ref, x_ref, wg_ref, wu_ref, wd_ref, o_ref, xb, acc):
    i = pl.program_id(0)
    jf = pl.program_id(1)

    @pl.when(i < nu_ref[0])
    def _():
        @pl.when(jf == 0)
        def _():
            xb[...] = x_ref[...].astype(BF16)
            acc[...] = jnp.zeros(acc.shape, F32)

        x = xb[...]
        gate = jnp.dot(x, wg_ref[...], preferred_element_type=F32)
        up = jnp.dot(x, wu_ref[...], preferred_element_type=F32)
        hm = (_silu(gate) * up).astype(BF16)
        acc[...] += jnp.dot(hm, wd_ref[...], preferred_element_type=F32)

        @pl.when(jf == pl.num_programs(1) - 1)
        def _():
            o_ref[...] = acc[...]


def _expert_ffn(tile_expert, n_used, xs, w_gu, w_down, tg, j):
    rows, d = xs.shape
    f = w_down.shape[2]
    fc = 512 if f % 512 == 0 else f
    nf = f // fc

    def row_map(i, jf, te, nu):
        return (jnp.minimum(i, nu[0] - 1), 0)

    def w_map(col0):
        def index_map(i, jf, te, nu):
            last = nu[0] - 1
            return (j, te[jnp.minimum(i, last)], 0, col0 + jnp.where(i <= last, jf, nf - 1))
        return index_map

    def wd_map(i, jf, te, nu):
        last = nu[0] - 1
        return (j, te[jnp.minimum(i, last)], jnp.where(i <= last, jf, nf - 1), 0)

    return pl.pallas_call(
        _gmm_kernel,
        name="moe_expert_swiglu",
        grid_spec=pltpu.PrefetchScalarGridSpec(
            num_scalar_prefetch=2,
            grid=(rows // tg, nf),
            in_specs=[pl.BlockSpec((tg, d), row_map),
                      pl.BlockSpec((None, None, d, fc), w_map(0)),
                      pl.BlockSpec((None, None, d, fc), w_map(nf)),
                      pl.BlockSpec((None, None, fc, d), wd_map)],
            out_specs=pl.BlockSpec((tg, d), row_map),
            scratch_shapes=[pltpu.VMEM((tg, d), BF16), pltpu.VMEM((tg, d), F32)]),
        out_shape=jax.ShapeDtypeStruct((rows, d), F32),
        compiler_params=_params("arbitrary", "arbitrary"),
    )(tile_expert, n_used, xs, w_gu, w_gu, w_down)


def _combine_kernel(src_ref, rw_ref, x_ref, mod_ref, ng_ref, ys_ref, o_ref, ybuf, sem, *, tm):
    def issue(g, carry):
        for u in range(ROW_UNROLL):
            t = g * ROW_UNROLL + u
            for k in range(2):
                _row_copy(ys_ref, src_ref[2 * t + k], ybuf.at[k], t, sem).start()
        return carry

    lax.fori_loop(0, tm // ROW_UNROLL, issue, 0)

    def drain(g, carry):
        for _ in range(2 * ROW_UNROLL):
            _row_copy(ys_ref, 0, ybuf.at[0], 0, sem).wait()
        return carry

    lax.fori_loop(0, tm // ROW_UNROLL, drain, 0)

    m = mod_ref[0, 0]
    ng = ng_ref[0]
    rw = rw_ref[0]
    y = rw[:, 0:1] * ybuf[0] + rw[:, 1:2] * ybuf[1]
    o_ref[0] = x_ref[0] + _row(m, GATE2) * _rms(y, _row(ng, 3))


def _combine_layer(dest_flat, rw, x, mod, norm_g, ys, layer):
    b, s, d = x.shape
    tm = min(512, s)
    nt = s // tm
    return pl.pallas_call(
        functools.partial(_combine_kernel, tm=tm),
        name="moe_combine",
        grid=(b, nt),
        in_specs=[pl.BlockSpec((2 * tm,), lambda i, t: (i * nt + t,), memory_space=pltpu.SMEM),
                  pl.BlockSpec((1, tm, LANES), lambda i, t: (i, t, 0)),
                  pl.BlockSpec((1, tm, d), lambda i, t: (i, t, 0)),
                  pl.BlockSpec((1, 1, N_MOD, d), lambda i, t: (layer, i, 0, 0)),
                  pl.BlockSpec((1, 4, d), lambda i, t: (layer, 0, 0)),
                  pl.BlockSpec(memory_space=pl.ANY)],
        out_specs=pl.BlockSpec((1, tm, d), lambda i, t: (i, t, 0)),
        scratch_shapes=[pltpu.VMEM((2, tm, d), F32), pltpu.SemaphoreType.DMA(())],
        out_shape=jax.ShapeDtypeStruct(x.shape, F32),
        compiler_params=_params("arbitrary", "arbitrary"),
    )(dest_flat, rw, x, mod, norm_g, ys)


def _moe_layer(h2, ri, rw, counts, xmid, mod, norm_g, w_gu, w_down, layer, j):
    b, s, d = xmid.shape
    n = b * s
    tg = min(1024, n)
    cnt = counts[0, :N_EXPERTS].astype(jnp.int32)
    padded = (cnt + tg - 1) // tg * tg
    offs = jnp.concatenate([jnp.zeros((1,), jnp.int32), jnp.cumsum(padded)]).astype(jnp.int32)
    n_tiles = 2 * n // tg + N_EXPERTS
    starts = jnp.arange(n_tiles, dtype=jnp.int32) * tg
    tile_expert = jnp.minimum(jnp.sum(starts[:, None] >= offs[None, 1:], axis=1),
                              N_EXPERTS - 1).astype(jnp.int32)
    n_used = (offs[N_EXPERTS:] // tg).astype(jnp.int32)
    sel = ri[:, :, 0:2]
    rank = ri[:, :, 2:4]
    starts_of = jnp.sum(jnp.where(sel[..., None] == jnp.arange(N_EXPERTS, dtype=jnp.int32),
                                  offs[:N_EXPERTS], 0), axis=-1)
    dest_flat = (starts_of + rank).reshape(-1).astype(jnp.int32)

    xs = _scatter_rows(offs, dest_flat, h2.reshape(n, d), n_tiles * tg, tg)
    ys = _expert_ffn(tile_expert, n_used, xs, w_gu, w_down, tg, j)
    return _combine_layer(dest_flat, rw, xmid, mod, norm_g, ys, layer)


def kernel(x, c, w_mod, b_mod, norm_g, pool_w, pool_scale, w_qkv, w_o, subln_g, lambda_vecs,
           rel_bias, ffn_w_gu, ffn_w_down, w_router, moe_w_gu, moe_w_down):
    depth = w_mod.shape[0]
    s = x.shape[1]
    d = x.shape[2]
    assert d == 2 * N_HEADS * HEAD_DIM
    t_attn = min(256, s)

    mod = _modulation(c, w_mod, b_mod)
    pool_w16 = pool_w.astype(BF16)
    ffn_gu16 = ffn_w_gu.astype(BF16)
    ffn_down16 = ffn_w_down.astype(BF16)
    wqk16 = w_qkv[:, :, :2 * d].astype(BF16)
    wvt16 = jnp.swapaxes(w_qkv[:, :, 2 * d:], 1, 2).astype(BF16)
    wo16 = w_o.astype(BF16)
    moe_gu16 = moe_w_gu.astype(BF16)
    moe_down16 = moe_w_down.astype(BF16)
    router_pad = jnp.pad(w_router, ((0, 0), (0, 0), (0, LANES - N_EXPERTS)))
    bias = _bias_tiles(rel_bias, t_attn)

    for i in range(depth):
        j = i // 2
        if i % 2 == 0:
            x = _pool_layer(x, mod, norm_g, pool_w16, pool_scale, i, j)
            x = _ffn_layer(x, mod, norm_g, ffn_gu16, ffn_down16, i, j)
        else:
            lam_init = 0.8 - 0.6 * math.exp(-0.3 * i)
            qk, vt = _qkv_layer(x, mod, norm_g, wqk16, wvt16, i, j, t_attn)
            o = _attention(qk, vt, bias, lambda_vecs, subln_g, j, lam_init, t_attn)
            xmid, h2, ri, rw, counts = _wo_route_layer(o, x, mod, norm_g, wo16, router_pad, i, j)
            x = _moe_layer(h2, ri, rw, counts, xmid, mod, norm_g, moe_gu16, moe_down16, i, j)
    return x
```

```python
import functools
import math

import jax
import jax.numpy as jnp
from jax import lax
from jax.experimental import pallas as pl
from jax.experimental.pallas import tpu as pltpu

F32 = jnp.float32
BF16 = jnp.bfloat16

EPS = 1e-6
N_MOD = 6
POOL_WINDOWS = (2, 4, 8, 16)
POOL_HALO = 16
N_HEADS = 8
HEAD_DIM = 64
V_HEAD_DIM = 2 * HEAD_DIM
N_BUCKETS = 32
MAX_DISTANCE = 128
N_EXPERTS = 8
LANES = 128
LOG2E = math.log2(math.e)
NEG_BIG = -1e30
MASKED_TILE = 2 * NEG_BIG
TILE_DIAG, TILE_SUB, TILE_FAR, TILE_MASKED = range(4)
N_TILE_KINDS = 4
ONES_ROWS = 16
VMEM_LIMIT = 56 * 1024 * 1024

SHIFT1, SCALE1, GATE1, SHIFT2, SCALE2, GATE2 = range(N_MOD)


def _params(*sem):
    return pltpu.CompilerParams(dimension_semantics=sem, vmem_limit_bytes=VMEM_LIMIT)


def _rms(x, g):
    return x * lax.rsqrt(jnp.mean(x * x, axis=-1, keepdims=True) + EPS) * g


def _norm_mod(x, g, scale, shift):
    return _rms(x, g) * (1.0 + scale) + shift


def _silu(x):
    return x * (1.0 / (1.0 + jnp.exp(-x)))


def _row(a, i):
    return a[i:i + 1, :]


def _mod_kernel(c_ref, w_ref, b_ref, o_ref):
    c = _silu(c_ref[...])
    o_ref[0] = jnp.dot(c, w_ref[0], precision=lax.Precision.HIGHEST,
                       preferred_element_type=F32) + b_ref[0]


def _modulation(c, w_mod, b_mod):
    depth, d, e = w_mod.shape
    b = c.shape[0]
    nc = 1536
    out = pl.pallas_call(
        _mod_kernel,
        name="adaln_mod",
        grid=(depth, e // nc),
        in_specs=[pl.BlockSpec((b, d), lambda l, j: (0, 0)),
                  pl.BlockSpec((1, d, nc), lambda l, j: (l, 0, j)),
                  pl.BlockSpec((1, 1, nc), lambda l, j: (l, 0, j))],
        out_specs=pl.BlockSpec((1, b, nc), lambda l, j: (l, 0, j)),
        out_shape=jax.ShapeDtypeStruct((depth, b, e), F32),
        compiler_params=_params("parallel", "parallel"),
    )(c, w_mod, b_mod.reshape(depth, 1, e))
    return out.reshape(depth, b, N_MOD, d)


def _pool_kernel(x_ref, mod_ref, ng_ref, pw_ref, ps_ref, o_ref, hbuf, *, ts):
    s = pl.program_id(1)
    x = x_ref[0]
    m = mod_ref[0, 0]
    ng = ng_ref[0]
    d = x.shape[-1]
    cg = d // len(POOL_WINDOWS)
    h = _norm_mod(x, _row(ng, 0), _row(m, SCALE1), _row(m, SHIFT1))

    @pl.when(s == 0)
    def _():
        hbuf[0:POOL_HALO, :] = jnp.zeros((POOL_HALO, d), F32)

    @pl.when(s > 0)
    def _():
        hbuf[0:POOL_HALO, :] = hbuf[ts:ts + POOL_HALO, :]

    hbuf[POOL_HALO:POOL_HALO + ts, :] = h
    count = s * ts + lax.broadcasted_iota(jnp.int32, (ts, 1), 0) + 1
    outs = []
    for g, w in enumerate(POOL_WINDOWS):
        lo = g * cg
        hg = h[:, lo:lo + cg]
        acc = hg
        for k in range(1, w):
            acc = acc + hbuf[POOL_HALO - k:POOL_HALO - k + ts, lo:lo + cg]
        pooled = acc / jnp.minimum(count, w).astype(F32) - hg
        outs.append(jnp.dot(pooled.astype(BF16), pw_ref[0, g], preferred_element_type=F32))
    y = jnp.concatenate(outs, axis=-1) * ps_ref[0]
    o_ref[0] = x + _row(m, GATE1) * _rms(y, _row(ng, 1))


def _pool_layer(x, mod, norm_g, pool_w, pool_scale, layer, j):
    b, s, d = x.shape
    ts = min(512, s)
    g, cg, _ = pool_w.shape[1:]
    return pl.pallas_call(
        functools.partial(_pool_kernel, ts=ts),
        name="pool_mixer",
        grid=(b, s // ts),
        in_specs=[pl.BlockSpec((1, ts, d), lambda i, t: (i, t, 0)),
                  pl.BlockSpec((1, 1, N_MOD, d), lambda i, t: (layer, i, 0, 0)),
                  pl.BlockSpec((1, 4, d), lambda i, t: (layer, 0, 0)),
                  pl.BlockSpec((1, g, cg, cg), lambda i, t: (j, 0, 0, 0)),
                  pl.BlockSpec((1, 1, d), lambda i, t: (j, 0, 0))],
        out_specs=pl.BlockSpec((1, ts, d), lambda i, t: (i, t, 0)),
        out_shape=jax.ShapeDtypeStruct(x.shape, F32),
        scratch_shapes=[pltpu.VMEM((POOL_HALO + ts, d), F32)],
        compiler_params=_params("arbitrary", "arbitrary"),
    )(x, mod, norm_g, pool_w, pool_scale.reshape(-1, 1, d))


def _ffn_kernel(x_ref, mod_ref, ng_ref, wgu_ref, wd_ref, o_ref, *, f, fc):
    x = x_ref[0]
    m = mod_ref[0, 0]
    ng = ng_ref[0]
    h = _norm_mod(x, _row(ng, 2), _row(m, SCALE2), _row(m, SHIFT2)).astype(BF16)
    acc = jnp.zeros(x.shape, F32)
    for c in range(f // fc):
        gate = jnp.dot(h, wgu_ref[0, :, c * fc:(c + 1) * fc], preferred_element_type=F32)
        up = jnp.dot(h, wgu_ref[0, :, f + c * fc:f + (c + 1) * fc], preferred_element_type=F32)
        hm = (_silu(gate) * up).astype(BF16)
        acc = acc + jnp.dot(hm, wd_ref[0, c * fc:(c + 1) * fc, :], preferred_element_type=F32)
    o_ref[0] = x + _row(m, GATE2) * _rms(acc, _row(ng, 3))


def _ffn_layer(x, mod, norm_g, w_gu, w_down, layer, j):
    b, s, d = x.shape
    f = w_down.shape[1]
    tm = min(512, s)
    fc = 256 if f % 256 == 0 else f
    return pl.pallas_call(
        functools.partial(_ffn_kernel, f=f, fc=fc),
        name="dense_swiglu",
        grid=(b, s // tm),
        in_specs=[pl.BlockSpec((1, tm, d), lambda i, t: (i, t, 0)),
                  pl.BlockSpec((1, 1, N_MOD, d), lambda i, t: (layer, i, 0, 0)),
                  pl.BlockSpec((1, 4, d), lambda i, t: (layer, 0, 0)),
                  pl.BlockSpec((1, d, 2 * f), lambda i, t: (j, 0, 0)),
                  pl.BlockSpec((1, f, d), lambda i, t: (j, 0, 0))],
        out_specs=pl.BlockSpec((1, tm, d), lambda i, t: (i, t, 0)),
        out_shape=jax.ShapeDtypeStruct(x.shape, F32),
        compiler_params=_params("parallel", "parallel"),
    )(x, mod, norm_g, w_gu, w_down)


def _qkv_kernel(x_ref, mod_ref, ng_ref, wqk_ref, wvt_ref, qk_ref, vt_ref, *, t):
    x = x_ref[0]
    m = mod_ref[0, 0]
    ng = ng_ref[0]
    d = x.shape[-1]
    h = _norm_mod(x, _row(ng, 0), _row(m, SCALE1), _row(m, SHIFT1)).astype(BF16)
    for c in range(2):
        r = jnp.dot(h, wqk_ref[0, :, c * d:(c + 1) * d], preferred_element_type=F32)
        if c == 0:
            r = r * (HEAD_DIM ** -0.5 * LOG2E)
        qk_ref[0, :, c * d:(c + 1) * d] = r.astype(BF16)
    vt = lax.dot_general(wvt_ref[0], h, (((1,), (1,)), ((), ())), preferred_element_type=F32)
    for kk in range(x.shape[0] // t):
        vt_ref[0, kk] = vt[:, kk * t:(kk + 1) * t].astype(BF16)


def _qkv_layer(x, mod, norm_g, w_qk, w_vt, layer, j, t):
    b, s, d = x.shape
    tm = min(512, s)
    return pl.pallas_call(
        functools.partial(_qkv_kernel, t=t),
        name="qkv_proj",
        grid=(b, s // tm),
        in_specs=[pl.BlockSpec((1, tm, d), lambda i, u: (i, u, 0)),
                  pl.BlockSpec((1, 1, N_MOD, d), lambda i, u: (layer, i, 0, 0)),
                  pl.BlockSpec((1, 4, d), lambda i, u: (layer, 0, 0)),
                  pl.BlockSpec((1, d, 2 * d), lambda i, u: (j, 0, 0)),
                  pl.BlockSpec((1, d, d), lambda i, u: (j, 0, 0))],
        out_specs=[pl.BlockSpec((1, tm, 2 * d), lambda i, u: (i, u, 0)),
                   pl.BlockSpec((1, tm // t, d, t), lambda i, u: (i, u, 0, 0))],
        out_shape=[jax.ShapeDtypeStruct((b, s, 2 * d), BF16),
                   jax.ShapeDtypeStruct((b, s // t, d, t), BF16)],
        compiler_params=_params("parallel", "parallel"),
    )(x, mod, norm_g, w_qk, w_vt)


def _bias_kernel(rb_ref, o_ref, *, t):
    h = pl.program_id(0)
    kind = pl.program_id(1)
    key = lax.broadcasted_iota(jnp.int32, (t, t), 0)
    qry = lax.broadcasted_iota(jnp.int32, (t, t), 1)
    n_signed = kind * t + qry - key
    n = jnp.maximum(n_signed, 0)
    max_exact = N_BUCKETS // 2
    nf = jnp.maximum(n, 1).astype(F32)
    large = max_exact + (jnp.log(nf / max_exact) / math.log(MAX_DISTANCE / max_exact)
                         * (N_BUCKETS - max_exact)).astype(jnp.int32)
    large = jnp.minimum(large, N_BUCKETS - 1)
    bucket = jnp.where(n < max_exact, n, large)
    bias = jnp.zeros((t, t), F32)
    for b in range(N_BUCKETS):
        bias = jnp.where(bucket == b, rb_ref[b * N_HEADS + h], bias)
    bias = jnp.where(n_signed < 0, NEG_BIG, bias * LOG2E)
    o_ref[0, 0] = jnp.where(kind == TILE_MASKED, MASKED_TILE, bias)


def _bias_tiles(rel_bias, t):
    return pl.pallas_call(
        functools.partial(_bias_kernel, t=t),
        name="t5_bias_tiles",
        grid=(N_HEADS, N_TILE_KINDS),
        in_specs=[pl.BlockSpec(memory_space=pltpu.SMEM)],
        out_specs=pl.BlockSpec((1, 1, t, t), lambda h, kind: (h, kind, 0, 0)),
        out_shape=jax.ShapeDtypeStruct((N_HEADS, N_TILE_KINDS, t, t), F32),
        compiler_params=_params("parallel", "parallel"),
    )(rel_bias.reshape(-1))


def _attn_kernel(q_ref, k_ref, vt_ref, bias_ref, lv_ref, sg_ref, o_ref,
                 qm, s0, s1, p0, p1, al0, al1, m_sc, acc_sc, *, t, nq, lam_init):
    m_sc[...] = jnp.full(m_sc.shape, NEG_BIG, F32)
    acc_sc[...] = jnp.zeros(acc_sc.shape, F32)
    s1[...] = jnp.full(s1.shape, MASKED_TILE, F32)
    p0[...] = jnp.zeros(p0.shape, BF16)
    al0[...] = jnp.ones(al0.shape, F32)
    ones = jnp.ones((ONES_ROWS, t), BF16)
    q_all = q_ref[0]
    lane = lax.broadcasted_iota(jnp.int32, q_all.shape, 1)
    zero = jnp.zeros_like(q_all)
    qm[0] = jnp.where(lane < HEAD_DIM, q_all, zero)
    qm[1] = jnp.where(lane >= HEAD_DIM, q_all, zero)
    last = nq - 1

    def advance(pair):
        ki, qi = pair
        wrap = qi == last
        return jnp.where(wrap, ki + 1, ki), jnp.where(wrap, ki + 1, qi + 1)

    def clipped(pair):
        return jnp.minimum(pair[0], last), jnp.minimum(pair[1], last)

    def scores(pair, s_out):
        kl, ql = clipped(pair)
        kind = jnp.where(pair[0] > last, TILE_MASKED, jnp.minimum(ql - kl, TILE_FAR))
        kt = k_ref[0, pl.ds(pl.multiple_of(kl * t, t), t), :]
        q_rows = pl.ds(pl.multiple_of(ql * t, t), t)
        bias = bias_ref[0, kind]
        for c in range(2):
            s_out[c] = lax.dot_general(kt, qm[c, q_rows, :], (((1,), (1,)), ((), ())),
                                       preferred_element_type=F32) + bias

    def softmax(pair, s_in, p_out, al_out):
        _, ql = clipped(pair)
        for c in range(2):
            sc = s_in[c]
            m_old = m_sc[ql, c]
            m_new = jnp.maximum(m_old, jnp.max(sc, axis=0, keepdims=True))
            p_out[c] = jnp.exp2(sc - m_new).astype(BF16)
            al_out[c] = jnp.exp2(m_old - m_new)
            m_sc[ql, c] = m_new

    def values(pair, p_in, al_in):
        kl, ql = clipped(pair)
        va = jnp.concatenate([vt_ref[0, kl], ones], axis=0)
        for c in range(2):
            acc_sc[ql, c] = (al_in[c] * acc_sc[ql, c]
                             + jnp.dot(va, p_in[c], preferred_element_type=F32))

    def tick_pair(_, carry):
        cur, lag1, lag2 = carry
        scores(cur, s0)
        softmax(lag1, s1, p1, al1)
        values(lag2, p0, al0)
        nxt = advance(cur)
        scores(nxt, s1)
        softmax(cur, s0, p0, al0)
        values(lag1, p1, al1)
        return advance(nxt), nxt, cur

    n_ticks = nq * (nq + 1) // 2 + 2
    start = (jnp.int32(0), jnp.int32(0))
    lax.fori_loop(0, (n_ticks + 1) // 2, tick_pair, (start, start, start))

    lv = lv_ref[0]
    lam = (jnp.exp(jnp.sum(_row(lv, 0) * _row(lv, 1), axis=-1, keepdims=True))
           - jnp.exp(jnp.sum(_row(lv, 2) * _row(lv, 3), axis=-1, keepdims=True)) + lam_init)
    gain = sg_ref[0] * (1.0 - lam_init)

    def finish(qi, carry):
        a0 = acc_sc[qi, 0]
        a1 = acc_sc[qi, 1]
        ot = (a0[:V_HEAD_DIM] / a0[V_HEAD_DIM:V_HEAD_DIM + 1]
              - lam * (a1[:V_HEAD_DIM] / a1[V_HEAD_DIM:V_HEAD_DIM + 1]))
        ot = ot * lax.rsqrt(jnp.mean(ot * ot, axis=0, keepdims=True) + EPS)
        o_ref[0, pl.ds(pl.multiple_of(qi * t, t), t), :] = (ot.T * gain).astype(BF16)
        return carry

    lax.fori_loop(0, nq, finish, 0)


def _attention(qk, vt, bias, lambda_vecs, subln_g, j, lam_init, t):
    b, s, d2 = qk.shape
    d = d2 // 2
    nq = s // t
    assert t >= MAX_DISTANCE and s % t == 0
    score_buf = pltpu.VMEM((2, t, t), F32)
    prob_buf = pltpu.VMEM((2, t, t), BF16)
    alpha_buf = pltpu.VMEM((2, 1, t), F32)
    head_cols = lambda base: pl.BlockSpec((1, s, V_HEAD_DIM), lambda i, h: (i, 0, base + h))
    return pl.pallas_call(
        functools.partial(_attn_kernel, t=t, nq=nq, lam_init=lam_init),
        name="diff_attention",
        grid=(b, N_HEADS),
        in_specs=[head_cols(0),
                  head_cols(N_HEADS),
                  pl.BlockSpec((1, nq, V_HEAD_DIM, t), lambda i, h: (i, 0, h, 0)),
                  pl.BlockSpec((1, N_TILE_KINDS, t, t), lambda i, h: (h, 0, 0, 0)),
                  pl.BlockSpec((1, 4, HEAD_DIM), lambda i, h: (j, 0, 0)),
                  pl.BlockSpec((1, 1, V_HEAD_DIM), lambda i, h: (j, 0, 0))],
        out_specs=head_cols(0),
        out_shape=jax.ShapeDtypeStruct((b, s, d), BF16),
        scratch_shapes=[pltpu.VMEM((2, s, V_HEAD_DIM), BF16),
                        score_buf, score_buf, prob_buf, prob_buf, alpha_buf, alpha_buf,
                        pltpu.VMEM((nq, 2, 1, t), F32),
                        pltpu.VMEM((nq, 2, V_HEAD_DIM + ONES_ROWS, t), F32)],
        compiler_params=_params("parallel", "parallel"),
    )(qk, qk, vt, bias, lambda_vecs, subln_g.reshape(-1, 1, V_HEAD_DIM))


def _wo_route_kernel(o_ref, x_ref, mod_ref, ng_ref, wo_ref, wr_ref,
                     xmid_ref, h2_ref, ri_ref, rw_ref, cnt_ref, run_sc, *, tm):
    first = jnp.logical_and(pl.program_id(0) == 0, pl.program_id(1) == 0)

    @pl.when(first)
    def _():
        run_sc[...] = jnp.zeros(run_sc.shape, F32)

    x = x_ref[0]
    m = mod_ref[0, 0]
    ng = ng_ref[0]
    y = jnp.dot(o_ref[0], wo_ref[0], preferred_element_type=F32)
    xn = x + _row(m, GATE1) * _rms(y, _row(ng, 1))
    xmid_ref[0] = xn
    h2 = _norm_mod(xn, _row(ng, 2), _row(m, SCALE2), _row(m, SHIFT2))
    h2_ref[0] = h2

    h_hi = h2.astype(BF16)
    h_lo = (h2 - h_hi.astype(F32)).astype(BF16)
    logits = (jnp.dot(h_hi, wr_ref[0, 0], preferred_element_type=F32)
              + jnp.dot(h_lo, wr_ref[0, 0], preferred_element_type=F32)
              + jnp.dot(h_hi, wr_ref[0, 1], preferred_element_type=F32))
    lane = lax.broadcasted_iota(jnp.int32, logits.shape, 1)
    lane_f = lane.astype(F32)
    neg_inf = jnp.float32(-jnp.inf)
    logits = jnp.where(lane < N_EXPERTS, logits, neg_inf)
    m1 = jnp.max(logits, axis=-1, keepdims=True)
    i1 = jnp.min(jnp.where(logits == m1, lane_f, float(LANES)), axis=-1, keepdims=True)
    oh1 = lane_f == i1
    rest = jnp.where(oh1, neg_inf, logits)
    m2 = jnp.max(rest, axis=-1, keepdims=True)
    i2 = jnp.min(jnp.where(rest == m2, lane_f, float(LANES)), axis=-1, keepdims=True)
    oh2 = lane_f == i2
    i1 = i1.astype(jnp.int32)
    i2 = i2.astype(jnp.int32)
    e21 = jnp.exp(m2 - m1)
    w1 = 1.0 / (1.0 + e21)
    w2 = e21 * w1

    oh = jnp.where(jnp.logical_or(oh1, oh2), 1.0, 0.0)
    r = lax.broadcasted_iota(jnp.int32, (tm, tm), 0)
    c = lax.broadcasted_iota(jnp.int32, (tm, tm), 1)
    tri = jnp.where(r > c, 1.0, 0.0).astype(BF16)
    prefix = jnp.dot(tri, oh.astype(BF16), preferred_element_type=F32) + run_sc[...]
    r1 = jnp.sum(jnp.where(oh1, prefix, 0.0), axis=-1, keepdims=True).astype(jnp.int32)
    r2 = jnp.sum(jnp.where(oh2, prefix, 0.0), axis=-1, keepdims=True).astype(jnp.int32)
    run_sc[...] = run_sc[...] + jnp.sum(oh, axis=0, keepdims=True)

    ri_ref[0] = jnp.where(lane == 0, i1, jnp.where(lane == 1, i2,
                          jnp.where(lane == 2, r1, jnp.where(lane == 3, r2, 0))))
    rw_ref[0] = jnp.where(lane == 0, w1, jnp.where(lane == 1, w2, 0.0))
    cnt_ref[...] = run_sc[...]


def _wo_route_layer(o, x, mod, norm_g, w_o, w_router, layer, j):
    b, s, d = x.shape
    tm = min(512, s)
    act = jax.ShapeDtypeStruct((b, s, d), F32)
    tile = pl.BlockSpec((1, tm, d), lambda i, t: (i, t, 0))
    lanes = pl.BlockSpec((1, tm, LANES), lambda i, t: (i, t, 0))
    return pl.pallas_call(
        functools.partial(_wo_route_kernel, tm=tm),
        name="attn_out_router",
        grid=(b, s // tm),
        in_specs=[tile, tile,
                  pl.BlockSpec((1, 1, N_MOD, d), lambda i, t: (layer, i, 0, 0)),
                  pl.BlockSpec((1, 4, d), lambda i, t: (layer, 0, 0)),
                  pl.BlockSpec((1, d, d), lambda i, t: (j, 0, 0)),
                  pl.BlockSpec((1, 2, d, LANES), lambda i, t: (j, 0, 0, 0))],
        out_specs=[tile, tile, lanes, lanes, pl.BlockSpec((1, LANES), lambda i, t: (0, 0))],
        out_shape=[act, act,
                   jax.ShapeDtypeStruct((b, s, LANES), jnp.int32),
                   jax.ShapeDtypeStruct((b, s, LANES), F32),
                   jax.ShapeDtypeStruct((1, LANES), F32)],
        scratch_shapes=[pltpu.VMEM((1, LANES), F32)],
        compiler_params=_params("arbitrary", "arbitrary"),
    )(o, x, mod, norm_g, w_o, w_router)


def _row_copy(src, src_row, dst, dst_row, sem):
    return pltpu.make_async_copy(src.at[pl.ds(src_row, 1)], dst.at[pl.ds(dst_row, 1)], sem)


ROW_UNROLL = 8


def _scatter_kernel(offs_ref, dst_ref, h_ref, xs_ref, zbuf, sem, *, tm, tg):
    @pl.when(pl.program_id(0) == 0)
    def _():
        zbuf[...] = jnp.zeros(zbuf.shape, F32)
        for e in range(N_EXPERTS):
            @pl.when(offs_ref[e + 1] > offs_ref[e])
            def _():
                start = pl.multiple_of(offs_ref[e + 1] - tg, tg)
                cp = pltpu.make_async_copy(zbuf, xs_ref.at[pl.ds(start, tg)], sem)
                cp.start()
                cp.wait()

    def issue(g, carry):
        for u in range(ROW_UNROLL):
            t = g * ROW_UNROLL + u
            for k in range(2):
                _row_copy(h_ref, t, xs_ref, dst_ref[2 * t + k], sem).start()
        return carry

    lax.fori_loop(0, tm // ROW_UNROLL, issue, 0)

    def drain(g, carry):
        for _ in range(2 * ROW_UNROLL):
            _row_copy(h_ref, 0, xs_ref, 0, sem).wait()
        return carry

    lax.fori_loop(0, tm // ROW_UNROLL, drain, 0)


def _scatter_rows(offs, dest_flat, h2, rows, tg):
    n, d = h2.shape
    tm = min(512, n)
    return pl.pallas_call(
        functools.partial(_scatter_kernel, tm=tm, tg=tg),
        name="moe_group_rows",
        grid_spec=pltpu.PrefetchScalarGridSpec(
            num_scalar_prefetch=1,
            grid=(n // tm,),
            in_specs=[pl.BlockSpec((2 * tm,), lambda i, offs: (i,), memory_space=pltpu.SMEM),
                      pl.BlockSpec((tm, d), lambda i, offs: (i, 0))],
            out_specs=pl.BlockSpec(memory_space=pl.ANY),
            scratch_shapes=[pltpu.VMEM((tg, d), F32), pltpu.SemaphoreType.DMA(())]),
        out_shape=jax.ShapeDtypeStruct((rows, d), F32),
        compiler_params=_params("arbitrary"),
    )(offs, dest_flat, h2)


def _gmm_kernel(te_ref, nu_ref, x_ref, wg_ref, wu_ref, wd_ref, o_ref, xb, acc, *, sub):
    i = pl.program_id(0)
    jf = pl.program_id(1)

    @pl.when(i < nu_ref[0])
    def _():
        @pl.when(jf == 0)
        def _():
            xb[...] = x_ref[...].astype(BF16)
            acc[...] = jnp.zeros(acc.shape, F32)

        x = xb[...]
        for c in range(wd_ref.shape[0] // sub):
            cols = slice(c * sub, (c + 1) * sub)
            gate = jnp.dot(x, wg_ref[:, cols], preferred_element_type=F32)
            up = jnp.dot(x, wu_ref[:, cols], preferred_element_type=F32)
            hm = (_silu(gate) * up).astype(BF16)
            acc[...] += jnp.dot(hm, wd_ref[cols, :], preferred_element_type=F32)

        @pl.when(jf == pl.num_programs(1) - 1)
        def _():
            o_ref[...] = acc[...]


def _expert_ffn(tile_expert, n_used, xs, w_gu, w_down, tg, j):
    rows, d = xs.shape
    f = w_down.shape[2]
    sub = 256 if f % 256 == 0 else f
    fc = f // 2 if f % (2 * sub) == 0 else f
    nf = f // fc

    def row_map(i, jf, te, nu):
        return (jnp.minimum(i, nu[0] - 1), 0)

    def w_map(col0):
        def index_map(i, jf, te, nu):
            last = nu[0] - 1
            return (j, te[jnp.minimum(i, last)], 0, col0 + jnp.where(i <= last, jf, nf - 1))
        return index_map

    def wd_map(i, jf, te, nu):
        last = nu[0] - 1
        return (j, te[jnp.minimum(i, last)], jnp.where(i <= last, jf, nf - 1), 0)

    return pl.pallas_call(
        functools.partial(_gmm_kernel, sub=sub),
        name="moe_expert_swiglu",
        grid_spec=pltpu.PrefetchScalarGridSpec(
            num_scalar_prefetch=2,
            grid=(rows // tg, nf),
            in_specs=[pl.BlockSpec((tg, d), row_map),
                      pl.BlockSpec((None, None, d, fc), w_map(0)),
                      pl.BlockSpec((None, None, d, fc), w_map(nf)),
                      pl.BlockSpec((None, None, fc, d), wd_map)],
            out_specs=pl.BlockSpec((tg, d), row_map),
            scratch_shapes=[pltpu.VMEM((tg, d), BF16), pltpu.VMEM((tg, d), F32)]),
        out_shape=jax.ShapeDtypeStruct((rows, d), F32),
        compiler_params=_params("arbitrary", "arbitrary"),
    )(tile_expert, n_used, xs, w_gu, w_gu, w_down)


def _combine_kernel(src_ref, rw_ref, x_ref, mod_ref, ng_ref, ys_ref, o_ref, ybuf, sem, *, tm):
    def issue(g, carry):
        for u in range(ROW_UNROLL):
            t = g * ROW_UNROLL + u
            for k in range(2):
                _row_copy(ys_ref, src_ref[2 * t + k], ybuf.at[k], t, sem).start()
        return carry

    lax.fori_loop(0, tm // ROW_UNROLL, issue, 0)

    def drain(g, carry):
        for _ in range(2 * ROW_UNROLL):
            _row_copy(ys_ref, 0, ybuf.at[0], 0, sem).wait()
        return carry

    lax.fori_loop(0, tm // ROW_UNROLL, drain, 0)

    m = mod_ref[0, 0]
    ng = ng_ref[0]
    rw = rw_ref[0]
    y = rw[:, 0:1] * ybuf[0] + rw[:, 1:2] * ybuf[1]
    o_ref[0] = x_ref[0] + _row(m, GATE2) * _rms(y, _row(ng, 3))


def _combine_layer(dest_flat, rw, x, mod, norm_g, ys, layer):
    b, s, d = x.shape
    tm = min(512, s)
    nt = s // tm
    return pl.pallas_call(
        functools.partial(_combine_kernel, tm=tm),
        name="moe_combine",
        grid=(b, nt),
        in_specs=[pl.BlockSpec((2 * tm,), lambda i, t: (i * nt + t,), memory_space=pltpu.SMEM),
                  pl.BlockSpec((1, tm, LANES), lambda i, t: (i, t, 0)),
                  pl.BlockSpec((1, tm, d), lambda i, t: (i, t, 0)),
                  pl.BlockSpec((1, 1, N_MOD, d), lambda i, t: (layer, i, 0, 0)),
                  pl.BlockSpec((1, 4, d), lambda i, t: (layer, 0, 0)),
                  pl.BlockSpec(memory_space=pl.ANY)],
        out_specs=pl.BlockSpec((1, tm, d), lambda i, t: (i, t, 0)),
        scratch_shapes=[pltpu.VMEM((2, tm, d), F32), pltpu.SemaphoreType.DMA(())],
        out_shape=jax.ShapeDtypeStruct(x.shape, F32),
        compiler_params=_params("arbitrary", "arbitrary"),
    )(dest_flat, rw, x, mod, norm_g, ys)


def _moe_layer(h2, ri, rw, counts, xmid, mod, norm_g, w_gu, w_down, layer, j):
    b, s, d = xmid.shape
    n = b * s
    tg = min(1024, n)
    cnt = counts[0, :N_EXPERTS].astype(jnp.int32)
    padded = (cnt + tg - 1) // tg * tg
    offs = jnp.concatenate([jnp.zeros((1,), jnp.int32), jnp.cumsum(padded)]).astype(jnp.int32)
    n_tiles = 2 * n // tg + N_EXPERTS
    starts = jnp.arange(n_tiles, dtype=jnp.int32) * tg
    tile_expert = jnp.minimum(jnp.sum(starts[:, None] >= offs[None, 1:], axis=1),
                              N_EXPERTS - 1).astype(jnp.int32)
    n_used = (offs[N_EXPERTS:] // tg).astype(jnp.int32)
    sel = ri[:, :, 0:2]
    rank = ri[:, :, 2:4]
    starts_of = jnp.sum(jnp.where(sel[..., None] == jnp.arange(N_EXPERTS, dtype=jnp.int32),
                                  offs[:N_EXPERTS], 0), axis=-1)
    dest_flat = (starts_of + rank).reshape(-1).astype(jnp.int32)

    xs = _scatter_rows(offs, dest_flat, h2.reshape(n, d), n_tiles * tg, tg)
    ys = _expert_ffn(tile_expert, n_used, xs, w_gu, w_down, tg, j)
    return _combine_layer(dest_flat, rw, xmid, mod, norm_g, ys, layer)


def kernel(x, c, w_mod, b_mod, norm_g, pool_w, pool_scale, w_qkv, w_o, subln_g, lambda_vecs,
           rel_bias, ffn_w_gu, ffn_w_down, w_router, moe_w_gu, moe_w_down):
    depth = w_mod.shape[0]
    s = x.shape[1]
    d = x.shape[2]
    assert d == 2 * N_HEADS * HEAD_DIM
    t_attn = min(256, s)

    mod = _modulation(c, w_mod, b_mod)
    pool_w16 = pool_w.astype(BF16)
    ffn_gu16 = ffn_w_gu.astype(BF16)
    ffn_down16 = ffn_w_down.astype(BF16)
    wqk16 = w_qkv[:, :, :2 * d].astype(BF16)
    wvt16 = jnp.swapaxes(w_qkv[:, :, 2 * d:], 1, 2).astype(BF16)
    wo16 = w_o.astype(BF16)
    moe_gu16 = moe_w_gu.astype(BF16)
    moe_down16 = moe_w_down.astype(BF16)
    router_pad = jnp.pad(w_router, ((0, 0), (0, 0), (0, LANES - N_EXPERTS)))
    router_hi = router_pad.astype(BF16)
    router_lo = (router_pad - router_hi.astype(F32)).astype(BF16)
    router_split = jnp.stack([router_hi, router_lo], axis=1)
    bias = _bias_tiles(rel_bias, t_attn)

    for i in range(depth):
        j = i // 2
        if i % 2 == 0:
            x = _pool_layer(x, mod, norm_g, pool_w16, pool_scale, i, j)
            x = _ffn_layer(x, mod, norm_g, ffn_gu16, ffn_down16, i, j)
        else:
            lam_init = 0.8 - 0.6 * math.exp(-0.3 * i)
            qk, vt = _qkv_layer(x, mod, norm_g, wqk16, wvt16, i, j, t_attn)
            o = _attention(qk, vt, bias, lambda_vecs, subln_g, j, lam_init, t_attn)
            xmid, h2, ri, rw, counts = _wo_route_layer(o, x, mod, norm_g, wo16, router_split, i, j)
            x = _moe_layer(h2, ri, rw, counts, xmid, mod, norm_g, moe_gu16, moe_down16, i, j)
    return x
```

```python
import functools
import math

import jax
import jax.numpy as jnp
from jax import lax
from jax.experimental import pallas as pl
from jax.experimental.pallas import tpu as pltpu

F32 = jnp.float32
BF16 = jnp.bfloat16

EPS = 1e-6
N_MOD = 6
POOL_WINDOWS = (2, 4, 8, 16)
POOL_HALO = 16
N_HEADS = 8
HEAD_DIM = 64
V_HEAD_DIM = 2 * HEAD_DIM
N_BUCKETS = 32
MAX_DISTANCE = 128
N_EXPERTS = 8
LANES = 128
LOG2E = math.log2(math.e)
NEG_BIG = -1e30
MASKED_TILE = 2 * NEG_BIG
TILE_DIAG, TILE_SUB, TILE_FAR, TILE_MASKED = range(4)
N_TILE_KINDS = 4
ONES_ROWS = 16
TICK_PAIRS_PER_ITER = 3
VMEM_LIMIT = 56 * 1024 * 1024

SHIFT1, SCALE1, GATE1, SHIFT2, SCALE2, GATE2 = range(N_MOD)


def _params(*sem):
    return pltpu.CompilerParams(dimension_semantics=sem, vmem_limit_bytes=VMEM_LIMIT)


def _rms(x, g):
    return x * lax.rsqrt(jnp.mean(x * x, axis=-1, keepdims=True) + EPS) * g


def _norm_mod(x, g, scale, shift):
    return _rms(x, g) * (1.0 + scale) + shift


def _silu(x):
    return x * (1.0 / (1.0 + jnp.exp(-x)))


def _row(a, i):
    return a[i:i + 1, :]


def _mod_kernel(c_ref, w_ref, b_ref, o_ref):
    c = _silu(c_ref[...])
    o_ref[0] = jnp.dot(c, w_ref[0], precision=lax.Precision.HIGHEST,
                       preferred_element_type=F32) + b_ref[0]


def _modulation(c, w_mod, b_mod):
    depth, d, e = w_mod.shape
    b = c.shape[0]
    nc = 1536
    out = pl.pallas_call(
        _mod_kernel,
        name="adaln_mod",
        grid=(depth, e // nc),
        in_specs=[pl.BlockSpec((b, d), lambda l, j: (0, 0)),
                  pl.BlockSpec((1, d, nc), lambda l, j: (l, 0, j)),
                  pl.BlockSpec((1, 1, nc), lambda l, j: (l, 0, j))],
        out_specs=pl.BlockSpec((1, b, nc), lambda l, j: (l, 0, j)),
        out_shape=jax.ShapeDtypeStruct((depth, b, e), F32),
        compiler_params=_params("parallel", "parallel"),
    )(c, w_mod, b_mod.reshape(depth, 1, e))
    return out.reshape(depth, b, N_MOD, d)


def _pool_kernel(x_ref, mod_ref, ng_ref, pw_ref, ps_ref, o_ref, hbuf, *, ts):
    s = pl.program_id(1)
    x = x_ref[0]
    m = mod_ref[0, 0]
    ng = ng_ref[0]
    d = x.shape[-1]
    cg = d // len(POOL_WINDOWS)
    h = _norm_mod(x, _row(ng, 0), _row(m, SCALE1), _row(m, SHIFT1))

    @pl.when(s == 0)
    def _():
        hbuf[0:POOL_HALO, :] = jnp.zeros((POOL_HALO, d), F32)

    @pl.when(s > 0)
    def _():
        hbuf[0:POOL_HALO, :] = hbuf[ts:ts + POOL_HALO, :]

    hbuf[POOL_HALO:POOL_HALO + ts, :] = h
    count = s * ts + lax.broadcasted_iota(jnp.int32, (ts, 1), 0) + 1
    outs = []
    for g, w in enumerate(POOL_WINDOWS):
        lo = g * cg
        hg = h[:, lo:lo + cg]
        acc = hg
        for k in range(1, w):
            acc = acc + hbuf[POOL_HALO - k:POOL_HALO - k + ts, lo:lo + cg]
        pooled = acc / jnp.minimum(count, w).astype(F32) - hg
        outs.append(jnp.dot(pooled.astype(BF16), pw_ref[0, g], preferred_element_type=F32))
    y = jnp.concatenate(outs, axis=-1) * ps_ref[0]
    o_ref[0] = x + _row(m, GATE1) * _rms(y, _row(ng, 1))


def _pool_layer(x, mod, norm_g, pool_w, pool_scale, layer, j):
    b, s, d = x.shape
    ts = min(512, s)
    g, cg, _ = pool_w.shape[1:]
    return pl.pallas_call(
        functools.partial(_pool_kernel, ts=ts),
        name="pool_mixer",
        grid=(b, s // ts),
        in_specs=[pl.BlockSpec((1, ts, d), lambda i, t: (i, t, 0)),
                  pl.BlockSpec((1, 1, N_MOD, d), lambda i, t: (layer, i, 0, 0)),
                  pl.BlockSpec((1, 4, d), lambda i, t: (layer, 0, 0)),
                  pl.BlockSpec((1, g, cg, cg), lambda i, t: (j, 0, 0, 0)),
                  pl.BlockSpec((1, 1, d), lambda i, t: (j, 0, 0))],
        out_specs=pl.BlockSpec((1, ts, d), lambda i, t: (i, t, 0)),
        out_shape=jax.ShapeDtypeStruct(x.shape, F32),
        scratch_shapes=[pltpu.VMEM((POOL_HALO + ts, d), F32)],
        compiler_params=_params("arbitrary", "arbitrary"),
    )(x, mod, norm_g, pool_w, pool_scale.reshape(-1, 1, d))


def _ffn_kernel(x_ref, mod_ref, ng_ref, wgu_ref, wd_ref, o_ref, *, f, fc):
    x = x_ref[0]
    m = mod_ref[0, 0]
    ng = ng_ref[0]
    h = _norm_mod(x, _row(ng, 2), _row(m, SCALE2), _row(m, SHIFT2)).astype(BF16)
    acc = jnp.zeros(x.shape, F32)
    for c in range(f // fc):
        gate = jnp.dot(h, wgu_ref[0, :, c * fc:(c + 1) * fc], preferred_element_type=F32)
        up = jnp.dot(h, wgu_ref[0, :, f + c * fc:f + (c + 1) * fc], preferred_element_type=F32)
        hm = (_silu(gate) * up).astype(BF16)
        acc = acc + jnp.dot(hm, wd_ref[0, c * fc:(c + 1) * fc, :], preferred_element_type=F32)
    o_ref[0] = x + _row(m, GATE2) * _rms(acc, _row(ng, 3))


def _ffn_layer(x, mod, norm_g, w_gu, w_down, layer, j):
    b, s, d = x.shape
    f = w_down.shape[1]
    tm = min(512, s)
    fc = 256 if f % 256 == 0 else f
    return pl.pallas_call(
        functools.partial(_ffn_kernel, f=f, fc=fc),
        name="dense_swiglu",
        grid=(b, s // tm),
        in_specs=[pl.BlockSpec((1, tm, d), lambda i, t: (i, t, 0)),
                  pl.BlockSpec((1, 1, N_MOD, d), lambda i, t: (layer, i, 0, 0)),
                  pl.BlockSpec((1, 4, d), lambda i, t: (layer, 0, 0)),
                  pl.BlockSpec((1, d, 2 * f), lambda i, t: (j, 0, 0)),
                  pl.BlockSpec((1, f, d), lambda i, t: (j, 0, 0))],
        out_specs=pl.BlockSpec((1, tm, d), lambda i, t: (i, t, 0)),
        out_shape=jax.ShapeDtypeStruct(x.shape, F32),
        compiler_params=_params("parallel", "parallel"),
    )(x, mod, norm_g, w_gu, w_down)


def _qkv_kernel(x_ref, mod_ref, ng_ref, wk_ref, wqvt_ref, k_ref, qt_ref, vt_ref, *, t):
    x = x_ref[0]
    m = mod_ref[0, 0]
    ng = ng_ref[0]
    d = x.shape[-1]
    h = _norm_mod(x, _row(ng, 0), _row(m, SCALE1), _row(m, SHIFT1)).astype(BF16)
    k_ref[0] = jnp.dot(h, wk_ref[0], preferred_element_type=F32).astype(BF16)
    qvt = lax.dot_general(wqvt_ref[0], h, (((1,), (1,)), ((), ())), preferred_element_type=F32)
    for kk in range(x.shape[0] // t):
        cols = slice(kk * t, (kk + 1) * t)
        qt_ref[0, kk] = (qvt[:d, cols] * (HEAD_DIM ** -0.5 * LOG2E)).astype(BF16)
        vt_ref[0, kk] = qvt[d:, cols].astype(BF16)


def _qkv_layer(x, mod, norm_g, w_k, w_qvt, layer, j, t):
    b, s, d = x.shape
    tm = min(512, s)
    transposed = jax.ShapeDtypeStruct((b, s // t, d, t), BF16)
    transposed_spec = pl.BlockSpec((1, tm // t, d, t), lambda i, u: (i, u, 0, 0))
    return pl.pallas_call(
        functools.partial(_qkv_kernel, t=t),
        name="qkv_proj",
        grid=(b, s // tm),
        in_specs=[pl.BlockSpec((1, tm, d), lambda i, u: (i, u, 0)),
                  pl.BlockSpec((1, 1, N_MOD, d), lambda i, u: (layer, i, 0, 0)),
                  pl.BlockSpec((1, 4, d), lambda i, u: (layer, 0, 0)),
                  pl.BlockSpec((1, d, d), lambda i, u: (j, 0, 0)),
                  pl.BlockSpec((1, 2 * d, d), lambda i, u: (j, 0, 0))],
        out_specs=[pl.BlockSpec((1, tm, d), lambda i, u: (i, u, 0)),
                   transposed_spec, transposed_spec],
        out_shape=[jax.ShapeDtypeStruct((b, s, d), BF16), transposed, transposed],
        compiler_params=_params("parallel", "parallel"),
    )(x, mod, norm_g, w_k, w_qvt)


def _bias_kernel(rb_ref, o_ref, *, t):
    h = pl.program_id(0)
    kind = pl.program_id(1)
    key = lax.broadcasted_iota(jnp.int32, (t, t), 0)
    qry = lax.broadcasted_iota(jnp.int32, (t, t), 1)
    n_signed = kind * t + qry - key
    n = jnp.maximum(n_signed, 0)
    max_exact = N_BUCKETS // 2
    nf = jnp.maximum(n, 1).astype(F32)
    large = max_exact + (jnp.log(nf / max_exact) / math.log(MAX_DISTANCE / max_exact)
                         * (N_BUCKETS - max_exact)).astype(jnp.int32)
    large = jnp.minimum(large, N_BUCKETS - 1)
    bucket = jnp.where(n < max_exact, n, large)
    bias = jnp.zeros((t, t), F32)
    for b in range(N_BUCKETS):
        bias = jnp.where(bucket == b, rb_ref[b * N_HEADS + h], bias)
    bias = jnp.where(n_signed < 0, NEG_BIG, bias * LOG2E)
    o_ref[0, 0] = jnp.where(kind == TILE_MASKED, MASKED_TILE, bias)


def _bias_tiles(rel_bias, t):
    return pl.pallas_call(
        functools.partial(_bias_kernel, t=t),
        name="t5_bias_tiles",
        grid=(N_HEADS, N_TILE_KINDS),
        in_specs=[pl.BlockSpec(memory_space=pltpu.SMEM)],
        out_specs=pl.BlockSpec((1, 1, t, t), lambda h, kind: (h, kind, 0, 0)),
        out_shape=jax.ShapeDtypeStruct((N_HEADS, N_TILE_KINDS, t, t), F32),
        compiler_params=_params("parallel", "parallel"),
    )(rel_bias.reshape(-1))


def _attn_kernel(qt_ref, k_ref, vt_ref, bias_ref, lv_ref, sg_ref, o_ref,
                 qm, s0, s1, p0, p1, al0, al1, m_sc, acc_sc, *, t, nq, lam_init):
    m_sc[...] = jnp.full(m_sc.shape, NEG_BIG, F32)
    acc_sc[...] = jnp.zeros(acc_sc.shape, F32)
    s1[...] = jnp.full(s1.shape, MASKED_TILE, F32)
    p0[...] = jnp.zeros(p0.shape, BF16)
    al0[...] = jnp.ones(al0.shape, F32)
    ones = jnp.ones((ONES_ROWS, t), BF16)
    q_all = qt_ref[0]
    feat = lax.broadcasted_iota(jnp.int32, q_all.shape, 1)
    zero = jnp.zeros_like(q_all)
    qm[0] = jnp.where(feat < HEAD_DIM, q_all, zero)
    qm[1] = jnp.where(feat >= HEAD_DIM, q_all, zero)
    last = nq - 1

    def advance(pair):
        ki, qi = pair
        wrap = qi == last
        return jnp.where(wrap, ki + 1, ki), jnp.where(wrap, ki + 1, qi + 1)

    def clipped(pair):
        return jnp.minimum(pair[0], last), jnp.minimum(pair[1], last)

    def scores(pair, s_out):
        kl, ql = clipped(pair)
        kind = jnp.where(pair[0] > last, TILE_MASKED, jnp.minimum(ql - kl, TILE_FAR))
        kt = k_ref[0, pl.ds(pl.multiple_of(kl * t, t), t), :]
        bias = bias_ref[0, kind]
        for c in range(2):
            s_out[c] = jnp.dot(kt, qm[c, ql], preferred_element_type=F32) + bias

    def softmax(pair, s_in, p_out, al_out):
        _, ql = clipped(pair)
        for c in range(2):
            sc = s_in[c]
            m_old = m_sc[ql, c]
            m_new = jnp.maximum(m_old, jnp.max(sc, axis=0, keepdims=True))
            p_out[c] = jnp.exp2(sc - m_new).astype(BF16)
            al_out[c] = jnp.exp2(m_old - m_new)
            m_sc[ql, c] = m_new

    def values(pair, p_in, al_in):
        kl, ql = clipped(pair)
        va = jnp.concatenate([vt_ref[0, kl], ones], axis=0)
        for c in range(2):
            acc_sc[ql, c] = (al_in[c] * acc_sc[ql, c]
                             + jnp.dot(va, p_in[c], preferred_element_type=F32))

    def tick_pair(_, carry):
        cur, lag1, lag2 = carry
        scores(cur, s0)
        softmax(lag1, s1, p1, al1)
        values(lag2, p0, al0)
        nxt = advance(cur)
        scores(nxt, s1)
        softmax(cur, s0, p0, al0)
        values(lag1, p1, al1)
        return advance(nxt), nxt, cur

    def tick_group(i, carry):
        for _ in range(TICK_PAIRS_PER_ITER):
            carry = tick_pair(i, carry)
        return carry

    n_ticks = nq * (nq + 1) // 2 + 2
    per_iter = 2 * TICK_PAIRS_PER_ITER
    start = (jnp.int32(0), jnp.int32(0))
    lax.fori_loop(0, (n_ticks + per_iter - 1) // per_iter, tick_group, (start, start, start))

    lv = lv_ref[0]
    lam = (jnp.exp(jnp.sum(_row(lv, 0) * _row(lv, 1), axis=-1, keepdims=True))
           - jnp.exp(jnp.sum(_row(lv, 2) * _row(lv, 3), axis=-1, keepdims=True)) + lam_init)
    gain = sg_ref[0] * (1.0 - lam_init)

    def finish(qi, carry):
        a0 = acc_sc[qi, 0]
        a1 = acc_sc[qi, 1]
        ot = (a0[:V_HEAD_DIM] / a0[V_HEAD_DIM:V_HEAD_DIM + 1]
              - lam * (a1[:V_HEAD_DIM] / a1[V_HEAD_DIM:V_HEAD_DIM + 1]))
        ot = ot * lax.rsqrt(jnp.mean(ot * ot, axis=0, keepdims=True) + EPS)
        o_ref[0, pl.ds(pl.multiple_of(qi * t, t), t), :] = (ot.T * gain).astype(BF16)
        return carry

    lax.fori_loop(0, nq, finish, 0)


def _attention(k, qt, vt, bias, lambda_vecs, subln_g, j, lam_init, t):
    b, s, d = k.shape
    nq = s // t
    assert t >= MAX_DISTANCE and s % t == 0
    score_buf = pltpu.VMEM((2, t, t), F32)
    prob_buf = pltpu.VMEM((2, t, t), BF16)
    alpha_buf = pltpu.VMEM((2, 1, t), F32)
    head_cols = pl.BlockSpec((1, s, V_HEAD_DIM), lambda i, h: (i, 0, h))
    head_rows = pl.BlockSpec((1, nq, V_HEAD_DIM, t), lambda i, h: (i, 0, h, 0))
    return pl.pallas_call(
        functools.partial(_attn_kernel, t=t, nq=nq, lam_init=lam_init),
        name="diff_attention",
        grid=(b, N_HEADS),
        in_specs=[head_rows, head_cols, head_rows,
                  pl.BlockSpec((1, N_TILE_KINDS, t, t), lambda i, h: (h, 0, 0, 0)),
                  pl.BlockSpec((1, 4, HEAD_DIM), lambda i, h: (j, 0, 0)),
                  pl.BlockSpec((1, 1, V_HEAD_DIM), lambda i, h: (j, 0, 0))],
        out_specs=head_cols,
        out_shape=jax.ShapeDtypeStruct((b, s, d), BF16),
        scratch_shapes=[pltpu.VMEM((2, nq, V_HEAD_DIM, t), BF16),
                        score_buf, score_buf, prob_buf, prob_buf, alpha_buf, alpha_buf,
                        pltpu.VMEM((nq, 2, 1, t), F32),
                        pltpu.VMEM((nq, 2, V_HEAD_DIM + ONES_ROWS, t), F32)],
        compiler_params=_params("parallel", "parallel"),
    )(qt, k, vt, bias, lambda_vecs, subln_g.reshape(-1, 1, V_HEAD_DIM))


def _wo_route_kernel(o_ref, x_ref, mod_ref, ng_ref, wo_ref, wr_ref,
                     xmid_ref, h2_ref, ri_ref, rw_ref, cnt_ref, run_sc, *, tm):
    first = jnp.logical_and(pl.program_id(0) == 0, pl.program_id(1) == 0)

    @pl.when(first)
    def _():
        run_sc[...] = jnp.zeros(run_sc.shape, F32)

    x = x_ref[0]
    m = mod_ref[0, 0]
    ng = ng_ref[0]
    y = jnp.dot(o_ref[0], wo_ref[0], preferred_element_type=F32)
    xn = x + _row(m, GATE1) * _rms(y, _row(ng, 1))
    xmid_ref[0] = xn
    h2 = _norm_mod(xn, _row(ng, 2), _row(m, SCALE2), _row(m, SHIFT2))
    h2_ref[0] = h2

    h_hi = h2.astype(BF16)
    h_lo = (h2 - h_hi.astype(F32)).astype(BF16)
    logits = (jnp.dot(h_hi, wr_ref[0, 0], preferred_element_type=F32)
              + jnp.dot(h_lo, wr_ref[0, 0], preferred_element_type=F32)
              + jnp.dot(h_hi, wr_ref[0, 1], preferred_element_type=F32))
    lane = lax.broadcasted_iota(jnp.int32, logits.shape, 1)
    lane_f = lane.astype(F32)
    neg_inf = jnp.float32(-jnp.inf)
    logits = jnp.where(lane < N_EXPERTS, logits, neg_inf)
    m1 = jnp.max(logits, axis=-1, keepdims=True)
    i1 = jnp.min(jnp.where(logits == m1, lane_f, float(LANES)), axis=-1, keepdims=True)
    oh1 = lane_f == i1
    rest = jnp.where(oh1, neg_inf, logits)
    m2 = jnp.max(rest, axis=-1, keepdims=True)
    i2 = jnp.min(jnp.where(rest == m2, lane_f, float(LANES)), axis=-1, keepdims=True)
    oh2 = lane_f == i2
    i1 = i1.astype(jnp.int32)
    i2 = i2.astype(jnp.int32)
    e21 = jnp.exp(m2 - m1)
    w1 = 1.0 / (1.0 + e21)
    w2 = e21 * w1

    oh = jnp.where(jnp.logical_or(oh1, oh2), 1.0, 0.0)
    r = lax.broadcasted_iota(jnp.int32, (tm, tm), 0)
    c = lax.broadcasted_iota(jnp.int32, (tm, tm), 1)
    tri = jnp.where(r > c, 1.0, 0.0).astype(BF16)
    prefix = jnp.dot(tri, oh.astype(BF16), preferred_element_type=F32) + run_sc[...]
    r1 = jnp.sum(jnp.where(oh1, prefix, 0.0), axis=-1, keepdims=True).astype(jnp.int32)
    r2 = jnp.sum(jnp.where(oh2, prefix, 0.0), axis=-1, keepdims=True).astype(jnp.int32)
    run_sc[...] = run_sc[...] + jnp.sum(oh, axis=0, keepdims=True)

    ri_ref[0] = jnp.where(lane == 0, i1, jnp.where(lane == 1, i2,
                          jnp.where(lane == 2, r1, jnp.where(lane == 3, r2, 0))))
    rw_ref[0] = jnp.where(lane == 0, w1, jnp.where(lane == 1, w2, 0.0))
    cnt_ref[...] = run_sc[...]


def _wo_route_layer(o, x, mod, norm_g, w_o, w_router, layer, j):
    b, s, d = x.shape
    tm = min(512, s)
    act = jax.ShapeDtypeStruct((b, s, d), F32)
    tile = pl.BlockSpec((1, tm, d), lambda i, t: (i, t, 0))
    lanes = pl.BlockSpec((1, tm, LANES), lambda i, t: (i, t, 0))
    return pl.pallas_call(
        functools.partial(_wo_route_kernel, tm=tm),
        name="attn_out_router",
        grid=(b, s // tm),
        in_specs=[tile, tile,
                  pl.BlockSpec((1, 1, N_MOD, d), lambda i, t: (layer, i, 0, 0)),
                  pl.BlockSpec((1, 4, d), lambda i, t: (layer, 0, 0)),
                  pl.BlockSpec((1, d, d), lambda i, t: (j, 0, 0)),
                  pl.BlockSpec((1, 2, d, LANES), lambda i, t: (j, 0, 0, 0))],
        out_specs=[tile, tile, lanes, lanes, pl.BlockSpec((1, LANES), lambda i, t: (0, 0))],
        out_shape=[act, act,
                   jax.ShapeDtypeStruct((b, s, LANES), jnp.int32),
                   jax.ShapeDtypeStruct((b, s, LANES), F32),
                   jax.ShapeDtypeStruct((1, LANES), F32)],
        scratch_shapes=[pltpu.VMEM((1, LANES), F32)],
        compiler_params=_params("arbitrary", "arbitrary"),
    )(o, x, mod, norm_g, w_o, w_router)


def _row_copy(src, src_row, dst, dst_row, sem):
    return pltpu.make_async_copy(src.at[pl.ds(src_row, 1)], dst.at[pl.ds(dst_row, 1)], sem)


ROW_UNROLL = 8


def _scatter_kernel(offs_ref, dst_ref, h_ref, xs_ref, zbuf, sem, *, tm, tg):
    @pl.when(pl.program_id(0) == 0)
    def _():
        zbuf[...] = jnp.zeros(zbuf.shape, F32)
        for e in range(N_EXPERTS):
            @pl.when(offs_ref[e + 1] > offs_ref[e])
            def _():
                start = pl.multiple_of(offs_ref[e + 1] - tg, tg)
                cp = pltpu.make_async_copy(zbuf, xs_ref.at[pl.ds(start, tg)], sem)
                cp.start()
                cp.wait()

    def issue(g, carry):
        for u in range(ROW_UNROLL):
            t = g * ROW_UNROLL + u
            for k in range(2):
                _row_copy(h_ref, t, xs_ref, dst_ref[2 * t + k], sem).start()
        return carry

    lax.fori_loop(0, tm // ROW_UNROLL, issue, 0)

    def drain(g, carry):
        for _ in range(2 * ROW_UNROLL):
            _row_copy(h_ref, 0, xs_ref, 0, sem).wait()
        return carry

    lax.fori_loop(0, tm // ROW_UNROLL, drain, 0)


def _scatter_rows(offs, dest_flat, h2, rows, tg):
    n, d = h2.shape
    tm = min(512, n)
    return pl.pallas_call(
        functools.partial(_scatter_kernel, tm=tm, tg=tg),
        name="moe_group_rows",
        grid_spec=pltpu.PrefetchScalarGridSpec(
            num_scalar_prefetch=1,
            grid=(n // tm,),
            in_specs=[pl.BlockSpec((2 * tm,), lambda i, offs: (i,), memory_space=pltpu.SMEM),
                      pl.BlockSpec((tm, d), lambda i, offs: (i, 0))],
            out_specs=pl.BlockSpec(memory_space=pl.ANY),
            scratch_shapes=[pltpu.VMEM((tg, d), F32), pltpu.SemaphoreType.DMA(())]),
        out_shape=jax.ShapeDtypeStruct((rows, d), F32),
        compiler_params=_params("arbitrary"),
    )(offs, dest_flat, h2)


def _gmm_kernel(te_ref, nu_ref, x_ref, wg_ref, wu_ref, wd_ref, o_ref, xb, acc, *, sub):
    i = pl.program_id(0)
    jf = pl.program_id(1)

    @pl.when(i < nu_ref[0])
    def _():
        @pl.when(jf == 0)
        def _():
            xb[...] = x_ref[...].astype(BF16)
            acc[...] = jnp.zeros(acc.shape, F32)

        x = xb[...]
        for c in range(wd_ref.shape[0] // sub):
            cols = slice(c * sub, (c + 1) * sub)
            gate = jnp.dot(x, wg_ref[:, cols], preferred_element_type=F32)
            up = jnp.dot(x, wu_ref[:, cols], preferred_element_type=F32)
            hm = (_silu(gate) * up).astype(BF16)
            acc[...] += jnp.dot(hm, wd_ref[cols, :], preferred_element_type=F32)

        @pl.when(jf == pl.num_programs(1) - 1)
        def _():
            o_ref[...] = acc[...]


def _expert_ffn(tile_expert, n_used, xs, w_gu, w_down, tg, j):
    rows, d = xs.shape
    f = w_down.shape[2]
    sub = 256 if f % 256 == 0 else f
    fc = f // 2 if f % (2 * sub) == 0 else f
    nf = f // fc

    def row_map(i, jf, te, nu):
        return (jnp.minimum(i, nu[0] - 1), 0)

    def w_map(col0):
        def index_map(i, jf, te, nu):
            last = nu[0] - 1
            return (j, te[jnp.minimum(i, last)], 0, col0 + jnp.where(i <= last, jf, nf - 1))
        return index_map

    def wd_map(i, jf, te, nu):
        last = nu[0] - 1
        return (j, te[jnp.minimum(i, last)], jnp.where(i <= last, jf, nf - 1), 0)

    return pl.pallas_call(
        functools.partial(_gmm_kernel, sub=sub),
        name="moe_expert_swiglu",
        grid_spec=pltpu.PrefetchScalarGridSpec(
            num_scalar_prefetch=2,
            grid=(rows // tg, nf),
            in_specs=[pl.BlockSpec((tg, d), row_map),
                      pl.BlockSpec((None, None, d, fc), w_map(0)),
                      pl.BlockSpec((None, None, d, fc), w_map(nf)),
                      pl.BlockSpec((None, None, fc, d), wd_map)],
            out_specs=pl.BlockSpec((tg, d), row_map),
            scratch_shapes=[pltpu.VMEM((tg, d), BF16), pltpu.VMEM((tg, d), F32)]),
        out_shape=jax.ShapeDtypeStruct((rows, d), F32),
        compiler_params=_params("arbitrary", "arbitrary"),
    )(tile_expert, n_used, xs, w_gu, w_gu, w_down)


def _combine_kernel(src_ref, rw_ref, x_ref, mod_ref, ng_ref, ys_ref, o_ref, ybuf, sem, *, tm):
    def issue(g, carry):
        for u in range(ROW_UNROLL):
            t = g * ROW_UNROLL + u
            for k in range(2):
                _row_copy(ys_ref, src_ref[2 * t + k], ybuf.at[k], t, sem).start()
        return carry

    lax.fori_loop(0, tm // ROW_UNROLL, issue, 0)

    def drain(g, carry):
        for _ in range(2 * ROW_UNROLL):
            _row_copy(ys_ref, 0, ybuf.at[0], 0, sem).wait()
        return carry

    lax.fori_loop(0, tm // ROW_UNROLL, drain, 0)

    m = mod_ref[0, 0]
    ng = ng_ref[0]
    rw = rw_ref[0]
    y = rw[:, 0:1] * ybuf[0] + rw[:, 1:2] * ybuf[1]
    o_ref[0] = x_ref[0] + _row(m, GATE2) * _rms(y, _row(ng, 3))


def _combine_layer(dest_flat, rw, x, mod, norm_g, ys, layer):
    b, s, d = x.shape
    tm = min(512, s)
    nt = s // tm
    return pl.pallas_call(
        functools.partial(_combine_kernel, tm=tm),
        name="moe_combine",
        grid=(b, nt),
        in_specs=[pl.BlockSpec((2 * tm,), lambda i, t: (i * nt + t,), memory_space=pltpu.SMEM),
                  pl.BlockSpec((1, tm, LANES), lambda i, t: (i, t, 0)),
                  pl.BlockSpec((1, tm, d), lambda i, t: (i, t, 0)),
                  pl.BlockSpec((1, 1, N_MOD, d), lambda i, t: (layer, i, 0, 0)),
                  pl.BlockSpec((1, 4, d), lambda i, t: (layer, 0, 0)),
                  pl.BlockSpec(memory_space=pl.ANY)],
        out_specs=pl.BlockSpec((1, tm, d), lambda i, t: (i, t, 0)),
        scratch_shapes=[pltpu.VMEM((2, tm, d), F32), pltpu.SemaphoreType.DMA(())],
        out_shape=jax.ShapeDtypeStruct(x.shape, F32),
        compiler_params=_params("arbitrary", "arbitrary"),
    )(dest_flat, rw, x, mod, norm_g, ys)


def _moe_layer(h2, ri, rw, counts, xmid, mod, norm_g, w_gu, w_down, layer, j):
    b, s, d = xmid.shape
    n = b * s
    tg = min(1024, n)
    cnt = counts[0, :N_EXPERTS].astype(jnp.int32)
    padded = (cnt + tg - 1) // tg * tg
    offs = jnp.concatenate([jnp.zeros((1,), jnp.int32), jnp.cumsum(padded)]).astype(jnp.int32)
    n_tiles = 2 * n // tg + N_EXPERTS
    starts = jnp.arange(n_tiles, dtype=jnp.int32) * tg
    tile_expert = jnp.minimum(jnp.sum(starts[:, None] >= offs[None, 1:], axis=1),
                              N_EXPERTS - 1).astype(jnp.int32)
    n_used = (offs[N_EXPERTS:] // tg).astype(jnp.int32)
    sel = ri[:, :, 0:2]
    rank = ri[:, :, 2:4]
    starts_of = jnp.sum(jnp.where(sel[..., None] == jnp.arange(N_EXPERTS, dtype=jnp.int32),
                                  offs[:N_EXPERTS], 0), axis=-1)
    dest_flat = (starts_of + rank).reshape(-1).astype(jnp.int32)

    xs = _scatter_rows(offs, dest_flat, h2.reshape(n, d), n_tiles * tg, tg)
    ys = _expert_ffn(tile_expert, n_used, xs, w_gu, w_down, tg, j)
    return _combine_layer(dest_flat, rw, xmid, mod, norm_g, ys, layer)


def kernel(x, c, w_mod, b_mod, norm_g, pool_w, pool_scale, w_qkv, w_o, subln_g, lambda_vecs,
           rel_bias, ffn_w_gu, ffn_w_down, w_router, moe_w_gu, moe_w_down):
    depth = w_mod.shape[0]
    s = x.shape[1]
    d = x.shape[2]
    assert d == 2 * N_HEADS * HEAD_DIM
    t_attn = min(256, s)

    mod = _modulation(c, w_mod, b_mod)
    pool_w16 = pool_w.astype(BF16)
    ffn_gu16 = ffn_w_gu.astype(BF16)
    ffn_down16 = ffn_w_down.astype(BF16)
    wk16 = w_qkv[:, :, d:2 * d].astype(BF16)
    wqvt16 = jnp.swapaxes(jnp.concatenate([w_qkv[:, :, :d], w_qkv[:, :, 2 * d:]], axis=2),
                          1, 2).astype(BF16)
    wo16 = w_o.astype(BF16)
    moe_gu16 = moe_w_gu.astype(BF16)
    moe_down16 = moe_w_down.astype(BF16)
    router_pad = jnp.pad(w_router, ((0, 0), (0, 0), (0, LANES - N_EXPERTS)))
    router_hi = router_pad.astype(BF16)
    router_lo = (router_pad - router_hi.astype(F32)).astype(BF16)
    router_split = jnp.stack([router_hi, router_lo], axis=1)
    bias = _bias_tiles(rel_bias, t_attn)

    for i in range(depth):
        j = i // 2
        if i % 2 == 0:
            x = _pool_layer(x, mod, norm_g, pool_w16, pool_scale, i, j)
            x = _ffn_layer(x, mod, norm_g, ffn_gu16, ffn_down16, i, j)
        else:
            lam_init = 0.8 - 0.6 * math.exp(-0.3 * i)
            k, qt, vt = _qkv_layer(x, mod, norm_g, wk16, wqvt16, i, j, t_attn)
            o = _attention(k, qt, vt, bias, lambda_vecs, subln_g, j, lam_init, t_attn)
            xmid, h2, ri, rw, counts = _wo_route_layer(o, x, mod, norm_g, wo16, router_split, i, j)
            x = _moe_layer(h2, ri, rw, counts, xmid, mod, norm_g, moe_gu16, moe_down16, i, j)
    return x
```

```python
import functools
import math

import jax
import jax.numpy as jnp
from jax import lax
from jax.experimental import pallas as pl
from jax.experimental.pallas import tpu as pltpu

F32 = jnp.float32
BF16 = jnp.bfloat16

EPS = 1e-6
N_MOD = 6
POOL_WINDOWS = (2, 4, 8, 16)
POOL_HALO = 16
N_HEADS = 8
HEAD_DIM = 64
V_HEAD_DIM = 2 * HEAD_DIM
N_BUCKETS = 32
MAX_DISTANCE = 128
N_EXPERTS = 8
LANES = 128
LOG2E = math.log2(math.e)
NEG_BIG = -1e30
MASKED_TILE = 2 * NEG_BIG
TILE_DIAG, TILE_SUB, TILE_FAR, TILE_MASKED = range(4)
N_TILE_KINDS = 4
ONES_ROWS = 16
TICK_PAIRS_PER_ITER = 3
VMEM_LIMIT = 56 * 1024 * 1024

SHIFT1, SCALE1, GATE1, SHIFT2, SCALE2, GATE2 = range(N_MOD)


def _params(*sem):
    return pltpu.CompilerParams(dimension_semantics=sem, vmem_limit_bytes=VMEM_LIMIT)


def _rms(x, g):
    return x * lax.rsqrt(jnp.mean(x * x, axis=-1, keepdims=True) + EPS) * g


def _norm_mod(x, g, scale, shift):
    return _rms(x, g) * (1.0 + scale) + shift


def _silu(x):
    return x * (1.0 / (1.0 + jnp.exp(-x)))


def _row(a, i):
    return a[i:i + 1, :]


def _mod_kernel(c_ref, w_ref, b_ref, o_ref):
    c = _silu(c_ref[...])
    o_ref[0] = jnp.dot(c, w_ref[0], precision=lax.Precision.HIGHEST,
                       preferred_element_type=F32) + b_ref[0]


def _modulation(c, w_mod, b_mod):
    depth, d, e = w_mod.shape
    b = c.shape[0]
    nc = 1536
    out = pl.pallas_call(
        _mod_kernel,
        name="adaln_mod",
        grid=(depth, e // nc),
        in_specs=[pl.BlockSpec((b, d), lambda l, j: (0, 0)),
                  pl.BlockSpec((1, d, nc), lambda l, j: (l, 0, j)),
                  pl.BlockSpec((1, 1, nc), lambda l, j: (l, 0, j))],
        out_specs=pl.BlockSpec((1, b, nc), lambda l, j: (l, 0, j)),
        out_shape=jax.ShapeDtypeStruct((depth, b, e), F32),
        compiler_params=_params("arbitrary", "arbitrary"),
    )(c, w_mod, b_mod.reshape(depth, 1, e))
    return out.reshape(depth, b, N_MOD, d)


def _pool_kernel(x_ref, mod_ref, ng_ref, pw_ref, ps_ref, o_ref, hbuf, *, ts):
    s = pl.program_id(1)
    x = x_ref[0]
    m = mod_ref[0, 0]
    ng = ng_ref[0]
    d = x.shape[-1]
    cg = d // len(POOL_WINDOWS)
    h = _norm_mod(x, _row(ng, 0), _row(m, SCALE1), _row(m, SHIFT1))

    @pl.when(s == 0)
    def _():
        hbuf[0:POOL_HALO, :] = jnp.zeros((POOL_HALO, d), F32)

    @pl.when(s > 0)
    def _():
        hbuf[0:POOL_HALO, :] = hbuf[ts:ts + POOL_HALO, :]

    hbuf[POOL_HALO:POOL_HALO + ts, :] = h
    count = s * ts + lax.broadcasted_iota(jnp.int32, (ts, 1), 0) + 1
    outs = []
    for g, w in enumerate(POOL_WINDOWS):
        lo = g * cg
        hg = h[:, lo:lo + cg]
        acc = hg
        for k in range(1, w):
            acc = acc + hbuf[POOL_HALO - k:POOL_HALO - k + ts, lo:lo + cg]
        pooled = acc / jnp.minimum(count, w).astype(F32) - hg
        outs.append(jnp.dot(pooled.astype(BF16), pw_ref[0, g], preferred_element_type=F32))
    y = jnp.concatenate(outs, axis=-1) * ps_ref[0]
    o_ref[0] = x + _row(m, GATE1) * _rms(y, _row(ng, 1))


def _pool_layer(x, mod, norm_g, pool_w, pool_scale, layer, j):
    b, s, d = x.shape
    ts = min(512, s)
    g, cg, _ = pool_w.shape[1:]
    return pl.pallas_call(
        functools.partial(_pool_kernel, ts=ts),
        name="pool_mixer",
        grid=(b, s // ts),
        in_specs=[pl.BlockSpec((1, ts, d), lambda i, t: (i, t, 0)),
                  pl.BlockSpec((1, 1, N_MOD, d), lambda i, t: (layer, i, 0, 0)),
                  pl.BlockSpec((1, 4, d), lambda i, t: (layer, 0, 0)),
                  pl.BlockSpec((1, g, cg, cg), lambda i, t: (j, 0, 0, 0)),
                  pl.BlockSpec((1, 1, d), lambda i, t: (j, 0, 0))],
        out_specs=pl.BlockSpec((1, ts, d), lambda i, t: (i, t, 0)),
        out_shape=jax.ShapeDtypeStruct(x.shape, F32),
        scratch_shapes=[pltpu.VMEM((POOL_HALO + ts, d), F32)],
        compiler_params=_params("arbitrary", "arbitrary"),
    )(x, mod, norm_g, pool_w, pool_scale.reshape(-1, 1, d))


def _ffn_kernel(x_ref, mod_ref, ng_ref, wgu_ref, wd_ref, o_ref, *, f, fc):
    x = x_ref[0]
    m = mod_ref[0, 0]
    ng = ng_ref[0]
    h = _norm_mod(x, _row(ng, 2), _row(m, SCALE2), _row(m, SHIFT2)).astype(BF16)
    acc = jnp.zeros(x.shape, F32)
    for c in range(f // fc):
        gate = jnp.dot(h, wgu_ref[0, :, c * fc:(c + 1) * fc], preferred_element_type=F32)
        up = jnp.dot(h, wgu_ref[0, :, f + c * fc:f + (c + 1) * fc], preferred_element_type=F32)
        hm = (_silu(gate) * up).astype(BF16)
        acc = acc + jnp.dot(hm, wd_ref[0, c * fc:(c + 1) * fc, :], preferred_element_type=F32)
    o_ref[0] = x + _row(m, GATE2) * _rms(acc, _row(ng, 3))


def _ffn_layer(x, mod, norm_g, w_gu, w_down, layer, j):
    b, s, d = x.shape
    f = w_down.shape[1]
    tm = min(512, s)
    fc = 256 if f % 256 == 0 else f
    return pl.pallas_call(
        functools.partial(_ffn_kernel, f=f, fc=fc),
        name="dense_swiglu",
        grid=(b, s // tm),
        in_specs=[pl.BlockSpec((1, tm, d), lambda i, t: (i, t, 0)),
                  pl.BlockSpec((1, 1, N_MOD, d), lambda i, t: (layer, i, 0, 0)),
                  pl.BlockSpec((1, 4, d), lambda i, t: (layer, 0, 0)),
                  pl.BlockSpec((1, d, 2 * f), lambda i, t: (j, 0, 0)),
                  pl.BlockSpec((1, f, d), lambda i, t: (j, 0, 0))],
        out_specs=pl.BlockSpec((1, tm, d), lambda i, t: (i, t, 0)),
        out_shape=jax.ShapeDtypeStruct(x.shape, F32),
        compiler_params=_params("arbitrary", "arbitrary"),
    )(x, mod, norm_g, w_gu, w_down)


def _qkv_kernel(x_ref, mod_ref, ng_ref, wk_ref, wqvt_ref, k_ref, qt_ref, vt_ref, *, t):
    x = x_ref[0]
    m = mod_ref[0, 0]
    ng = ng_ref[0]
    d = x.shape[-1]
    h = _norm_mod(x, _row(ng, 0), _row(m, SCALE1), _row(m, SHIFT1)).astype(BF16)
    k_ref[0] = jnp.dot(h, wk_ref[0], preferred_element_type=F32).astype(BF16)
    qvt = lax.dot_general(wqvt_ref[0], h, (((1,), (1,)), ((), ())), preferred_element_type=F32)
    for kk in range(x.shape[0] // t):
        cols = slice(kk * t, (kk + 1) * t)
        qt_ref[0, kk] = (qvt[:d, cols] * (HEAD_DIM ** -0.5 * LOG2E)).astype(BF16)
        vt_ref[0, kk] = qvt[d:, cols].astype(BF16)


def _qkv_layer(x, mod, norm_g, w_k, w_qvt, layer, j, t):
    b, s, d = x.shape
    tm = min(512, s)
    transposed = jax.ShapeDtypeStruct((b, s // t, d, t), BF16)
    transposed_spec = pl.BlockSpec((1, tm // t, d, t), lambda i, u: (i, u, 0, 0))
    return pl.pallas_call(
        functools.partial(_qkv_kernel, t=t),
        name="qkv_proj",
        grid=(b, s // tm),
        in_specs=[pl.BlockSpec((1, tm, d), lambda i, u: (i, u, 0)),
                  pl.BlockSpec((1, 1, N_MOD, d), lambda i, u: (layer, i, 0, 0)),
                  pl.BlockSpec((1, 4, d), lambda i, u: (layer, 0, 0)),
                  pl.BlockSpec((1, d, d), lambda i, u: (j, 0, 0)),
                  pl.BlockSpec((1, 2 * d, d), lambda i, u: (j, 0, 0))],
        out_specs=[pl.BlockSpec((1, tm, d), lambda i, u: (i, u, 0)),
                   transposed_spec, transposed_spec],
        out_shape=[jax.ShapeDtypeStruct((b, s, d), BF16), transposed, transposed],
        compiler_params=_params("arbitrary", "arbitrary"),
    )(x, mod, norm_g, w_k, w_qvt)


def _bias_kernel(rb_ref, o_ref, *, t):
    h = pl.program_id(0)
    kind = pl.program_id(1)
    key = lax.broadcasted_iota(jnp.int32, (t, t), 0)
    qry = lax.broadcasted_iota(jnp.int32, (t, t), 1)
    n_signed = kind * t + qry - key
    n = jnp.maximum(n_signed, 0)
    max_exact = N_BUCKETS // 2
    nf = jnp.maximum(n, 1).astype(F32)
    large = max_exact + (jnp.log(nf / max_exact) / math.log(MAX_DISTANCE / max_exact)
                         * (N_BUCKETS - max_exact)).astype(jnp.int32)
    large = jnp.minimum(large, N_BUCKETS - 1)
    bucket = jnp.where(n < max_exact, n, large)
    bias = jnp.zeros((t, t), F32)
    for b in range(N_BUCKETS):
        bias = jnp.where(bucket == b, rb_ref[b * N_HEADS + h], bias)
    bias = jnp.where(n_signed < 0, NEG_BIG, bias * LOG2E)
    o_ref[0, 0] = jnp.where(kind == TILE_MASKED, MASKED_TILE, bias)


def _bias_tiles(rel_bias, t):
    return pl.pallas_call(
        functools.partial(_bias_kernel, t=t),
        name="t5_bias_tiles",
        grid=(N_HEADS, N_TILE_KINDS),
        in_specs=[pl.BlockSpec(memory_space=pltpu.SMEM)],
        out_specs=pl.BlockSpec((1, 1, t, t), lambda h, kind: (h, kind, 0, 0)),
        out_shape=jax.ShapeDtypeStruct((N_HEADS, N_TILE_KINDS, t, t), F32),
        compiler_params=_params("arbitrary", "arbitrary"),
    )(rel_bias.reshape(-1))


def _attn_kernel(qt_ref, k_ref, vt_ref, bias_ref, lv_ref, sg_ref, o_ref,
                 qm, s0, s1, p0, p1, al0, al1, m_sc, acc_sc, *, t, nq, lam_init):
    m_sc[...] = jnp.full(m_sc.shape, NEG_BIG, F32)
    acc_sc[...] = jnp.zeros(acc_sc.shape, F32)
    s1[...] = jnp.full(s1.shape, MASKED_TILE, F32)
    p0[...] = jnp.zeros(p0.shape, BF16)
    al0[...] = jnp.ones(al0.shape, F32)
    ones = jnp.ones((ONES_ROWS, t), BF16)
    q_all = qt_ref[0]
    feat = lax.broadcasted_iota(jnp.int32, q_all.shape, 1)
    zero = jnp.zeros_like(q_all)
    qm[0] = jnp.where(feat < HEAD_DIM, q_all, zero)
    qm[1] = jnp.where(feat >= HEAD_DIM, q_all, zero)
    last = nq - 1

    def advance(pair):
        ki, qi = pair
        wrap = qi == last
        return jnp.where(wrap, ki + 1, ki), jnp.where(wrap, ki + 1, qi + 1)

    def clipped(pair):
        return jnp.minimum(pair[0], last), jnp.minimum(pair[1], last)

    def scores(pair, s_out):
        kl, ql = clipped(pair)
        kind = jnp.where(pair[0] > last, TILE_MASKED, jnp.minimum(ql - kl, TILE_FAR))
        kt = k_ref[0, pl.ds(pl.multiple_of(kl * t, t), t), :]
        bias = bias_ref[0, kind]
        for c in range(2):
            s_out[c] = jnp.dot(kt, qm[c, ql], preferred_element_type=F32) + bias

    def softmax(pair, s_in, p_out, al_out):
        _, ql = clipped(pair)
        for c in range(2):
            sc = s_in[c]
            m_old = m_sc[ql, c]
            m_new = jnp.maximum(m_old, jnp.max(sc, axis=0, keepdims=True))
            p_out[c] = jnp.exp2(sc - m_new).astype(BF16)
            al_out[c] = jnp.exp2(m_old - m_new)
            m_sc[ql, c] = m_new

    def values(pair, p_in, al_in):
        kl, ql = clipped(pair)
        va = jnp.concatenate([vt_ref[0, kl], ones], axis=0)
        for c in range(2):
            acc_sc[ql, c] = (al_in[c] * acc_sc[ql, c]
                             + jnp.dot(va, p_in[c], preferred_element_type=F32))

    def tick_pair(_, carry):
        cur, lag1, lag2 = carry
        scores(cur, s0)
        softmax(lag1, s1, p1, al1)
        values(lag2, p0, al0)
        nxt = advance(cur)
        scores(nxt, s1)
        softmax(cur, s0, p0, al0)
        values(lag1, p1, al1)
        return advance(nxt), nxt, cur

    def tick_group(i, carry):
        for _ in range(TICK_PAIRS_PER_ITER):
            carry = tick_pair(i, carry)
        return carry

    n_ticks = nq * (nq + 1) // 2 + 2
    per_iter = 2 * TICK_PAIRS_PER_ITER
    start = (jnp.int32(0), jnp.int32(0))
    lax.fori_loop(0, (n_ticks + per_iter - 1) // per_iter, tick_group, (start, start, start))

    lv = lv_ref[0]
    lam = (jnp.exp(jnp.sum(_row(lv, 0) * _row(lv, 1), axis=-1, keepdims=True))
           - jnp.exp(jnp.sum(_row(lv, 2) * _row(lv, 3), axis=-1, keepdims=True)) + lam_init)
    gain = sg_ref[0] * (1.0 - lam_init)

    def finish(qi, carry):
        a0 = acc_sc[qi, 0]
        a1 = acc_sc[qi, 1]
        ot = (a0[:V_HEAD_DIM] / a0[V_HEAD_DIM:V_HEAD_DIM + 1]
              - lam * (a1[:V_HEAD_DIM] / a1[V_HEAD_DIM:V_HEAD_DIM + 1]))
        ot = ot * lax.rsqrt(jnp.mean(ot * ot, axis=0, keepdims=True) + EPS)
        o_ref[0, pl.ds(pl.multiple_of(qi * t, t), t), :] = (ot.T * gain).astype(BF16)
        return carry

    lax.fori_loop(0, nq, finish, 0)


def _attention(k, qt, vt, bias, lambda_vecs, subln_g, j, lam_init, t):
    b, s, d = k.shape
    nq = s // t
    assert t >= MAX_DISTANCE and s % t == 0
    score_buf = pltpu.VMEM((2, t, t), F32)
    prob_buf = pltpu.VMEM((2, t, t), BF16)
    alpha_buf = pltpu.VMEM((2, 1, t), F32)
    head_cols = pl.BlockSpec((1, s, V_HEAD_DIM), lambda i, h: (i, 0, h))
    head_rows = pl.BlockSpec((1, nq, V_HEAD_DIM, t), lambda i, h: (i, 0, h, 0))
    return pl.pallas_call(
        functools.partial(_attn_kernel, t=t, nq=nq, lam_init=lam_init),
        name="diff_attention",
        grid=(b, N_HEADS),
        in_specs=[head_rows, head_cols, head_rows,
                  pl.BlockSpec((1, N_TILE_KINDS, t, t), lambda i, h: (h, 0, 0, 0)),
                  pl.BlockSpec((1, 4, HEAD_DIM), lambda i, h: (j, 0, 0)),
                  pl.BlockSpec((1, 1, V_HEAD_DIM), lambda i, h: (j, 0, 0))],
        out_specs=head_cols,
        out_shape=jax.ShapeDtypeStruct((b, s, d), BF16),
        scratch_shapes=[pltpu.VMEM((2, nq, V_HEAD_DIM, t), BF16),
                        score_buf, score_buf, prob_buf, prob_buf, alpha_buf, alpha_buf,
                        pltpu.VMEM((nq, 2, 1, t), F32),
                        pltpu.VMEM((nq, 2, V_HEAD_DIM + ONES_ROWS, t), F32)],
        compiler_params=_params("arbitrary", "arbitrary"),
    )(qt, k, vt, bias, lambda_vecs, subln_g.reshape(-1, 1, V_HEAD_DIM))


def _wo_route_kernel(o_ref, x_ref, mod_ref, ng_ref, wo_ref, wr_ref,
                     xmid_ref, h2_ref, ri_ref, rw_ref, cnt_ref, run_sc, *, tm):
    first = jnp.logical_and(pl.program_id(0) == 0, pl.program_id(1) == 0)

    @pl.when(first)
    def _():
        run_sc[...] = jnp.zeros(run_sc.shape, F32)

    x = x_ref[0]
    m = mod_ref[0, 0]
    ng = ng_ref[0]
    y = jnp.dot(o_ref[0], wo_ref[0], preferred_element_type=F32)
    xn = x + _row(m, GATE1) * _rms(y, _row(ng, 1))
    xmid_ref[0] = xn
    h2 = _norm_mod(xn, _row(ng, 2), _row(m, SCALE2), _row(m, SHIFT2))
    h2_ref[0] = h2

    h_hi = h2.astype(BF16)
    h_lo = (h2 - h_hi.astype(F32)).astype(BF16)
    logits = (jnp.dot(h_hi, wr_ref[0, 0], preferred_element_type=F32)
              + jnp.dot(h_lo, wr_ref[0, 0], preferred_element_type=F32)
              + jnp.dot(h_hi, wr_ref[0, 1], preferred_element_type=F32))
    lane = lax.broadcasted_iota(jnp.int32, logits.shape, 1)
    lane_f = lane.astype(F32)
    neg_inf = jnp.float32(-jnp.inf)
    logits = jnp.where(lane < N_EXPERTS, logits, neg_inf)
    m1 = jnp.max(logits, axis=-1, keepdims=True)
    i1 = jnp.min(jnp.where(logits == m1, lane_f, float(LANES)), axis=-1, keepdims=True)
    oh1 = lane_f == i1
    rest = jnp.where(oh1, neg_inf, logits)
    m2 = jnp.max(rest, axis=-1, keepdims=True)
    i2 = jnp.min(jnp.where(rest == m2, lane_f, float(LANES)), axis=-1, keepdims=True)
    oh2 = lane_f == i2
    i1 = i1.astype(jnp.int32)
    i2 = i2.astype(jnp.int32)
    e21 = jnp.exp(m2 - m1)
    w1 = 1.0 / (1.0 + e21)
    w2 = e21 * w1

    oh = jnp.where(jnp.logical_or(oh1, oh2), 1.0, 0.0)
    r = lax.broadcasted_iota(jnp.int32, (tm, tm), 0)
    c = lax.broadcasted_iota(jnp.int32, (tm, tm), 1)
    tri = jnp.where(r > c, 1.0, 0.0).astype(BF16)
    prefix = jnp.dot(tri, oh.astype(BF16), preferred_element_type=F32) + run_sc[...]
    r1 = jnp.sum(jnp.where(oh1, prefix, 0.0), axis=-1, keepdims=True).astype(jnp.int32)
    r2 = jnp.sum(jnp.where(oh2, prefix, 0.0), axis=-1, keepdims=True).astype(jnp.int32)
    run_sc[...] = run_sc[...] + jnp.sum(oh, axis=0, keepdims=True)

    ri_ref[0] = jnp.where(lane == 0, i1, jnp.where(lane == 1, i2,
                          jnp.where(lane == 2, r1, jnp.where(lane == 3, r2, 0))))
    rw_ref[0] = jnp.where(lane == 0, w1, jnp.where(lane == 1, w2, 0.0))
    cnt_ref[...] = run_sc[...]


def _wo_route_layer(o, x, mod, norm_g, w_o, w_router, layer, j):
    b, s, d = x.shape
    tm = min(512, s)
    act = jax.ShapeDtypeStruct((b, s, d), F32)
    tile = pl.BlockSpec((1, tm, d), lambda i, t: (i, t, 0))
    lanes = pl.BlockSpec((1, tm, LANES), lambda i, t: (i, t, 0))
    return pl.pallas_call(
        functools.partial(_wo_route_kernel, tm=tm),
        name="attn_out_router",
        grid=(b, s // tm),
        in_specs=[tile, tile,
                  pl.BlockSpec((1, 1, N_MOD, d), lambda i, t: (layer, i, 0, 0)),
                  pl.BlockSpec((1, 4, d), lambda i, t: (layer, 0, 0)),
                  pl.BlockSpec((1, d, d), lambda i, t: (j, 0, 0)),
                  pl.BlockSpec((1, 2, d, LANES), lambda i, t: (j, 0, 0, 0))],
        out_specs=[tile, tile, lanes, lanes, pl.BlockSpec((1, LANES), lambda i, t: (0, 0))],
        out_shape=[act, act,
                   jax.ShapeDtypeStruct((b, s, LANES), jnp.int32),
                   jax.ShapeDtypeStruct((b, s, LANES), F32),
                   jax.ShapeDtypeStruct((1, LANES), F32)],
        scratch_shapes=[pltpu.VMEM((1, LANES), F32)],
        compiler_params=_params("arbitrary", "arbitrary"),
    )(o, x, mod, norm_g, w_o, w_router)


def _row_copy(src, src_row, dst, dst_row, sem):
    return pltpu.make_async_copy(src.at[pl.ds(src_row, 1)], dst.at[pl.ds(dst_row, 1)], sem)


ROW_UNROLL = 8
COMBINE_PARTS = 4


def _scatter_kernel(offs_ref, dst_ref, h_ref, xs_ref, zbuf, sem, *, tm, tg):
    @pl.when(pl.program_id(0) == 0)
    def _():
        zbuf[...] = jnp.zeros(zbuf.shape, F32)
        for e in range(N_EXPERTS):
            @pl.when(offs_ref[e + 1] > offs_ref[e])
            def _():
                start = pl.multiple_of(offs_ref[e + 1] - tg, tg)
                cp = pltpu.make_async_copy(zbuf, xs_ref.at[pl.ds(start, tg)], sem)
                cp.start()
                cp.wait()

    def issue(g, carry):
        for u in range(ROW_UNROLL):
            t = g * ROW_UNROLL + u
            for k in range(2):
                _row_copy(h_ref, t, xs_ref, dst_ref[2 * t + k], sem).start()
        return carry

    lax.fori_loop(0, tm // ROW_UNROLL, issue, 0)

    def drain(g, carry):
        for _ in range(2 * ROW_UNROLL):
            _row_copy(h_ref, 0, xs_ref, 0, sem).wait()
        return carry

    lax.fori_loop(0, tm // ROW_UNROLL, drain, 0)


def _scatter_rows(offs, dest_flat, h2, rows, tg):
    n, d = h2.shape
    tm = min(1024, n)
    return pl.pallas_call(
        functools.partial(_scatter_kernel, tm=tm, tg=tg),
        name="moe_group_rows",
        grid_spec=pltpu.PrefetchScalarGridSpec(
            num_scalar_prefetch=1,
            grid=(n // tm,),
            in_specs=[pl.BlockSpec((2 * tm,), lambda i, offs: (i,), memory_space=pltpu.SMEM),
                      pl.BlockSpec((tm, d), lambda i, offs: (i, 0))],
            out_specs=pl.BlockSpec(memory_space=pl.ANY),
            scratch_shapes=[pltpu.VMEM((tg, d), F32), pltpu.SemaphoreType.DMA(())]),
        out_shape=jax.ShapeDtypeStruct((rows, d), F32),
        compiler_params=_params("arbitrary"),
    )(offs, dest_flat, h2)


def _gmm_kernel(te_ref, nu_ref, x_ref, wg_ref, wu_ref, wd_ref, o_ref, xb, acc, *, sub):
    i = pl.program_id(0)
    jf = pl.program_id(1)

    @pl.when(i < nu_ref[0])
    def _():
        @pl.when(jf == 0)
        def _():
            xb[...] = x_ref[...].astype(BF16)
            acc[...] = jnp.zeros(acc.shape, F32)

        x = xb[...]
        for c in range(wd_ref.shape[0] // sub):
            cols = slice(c * sub, (c + 1) * sub)
            gate = jnp.dot(x, wg_ref[:, cols], preferred_element_type=F32)
            up = jnp.dot(x, wu_ref[:, cols], preferred_element_type=F32)
            hm = (_silu(gate) * up).astype(BF16)
            acc[...] += jnp.dot(hm, wd_ref[cols, :], preferred_element_type=F32)

        @pl.when(jf == pl.num_programs(1) - 1)
        def _():
            o_ref[...] = acc[...]


def _expert_ffn(tile_expert, n_used, xs, w_gu, w_down, tg, j):
    rows, d = xs.shape
    f = w_down.shape[2]
    sub = 256 if f % 256 == 0 else f
    fc = f // 2 if f % (2 * sub) == 0 else f
    nf = f // fc

    def row_map(i, jf, te, nu):
        return (jnp.minimum(i, nu[0] - 1), 0)

    def w_map(col0):
        def index_map(i, jf, te, nu):
            last = nu[0] - 1
            return (j, te[jnp.minimum(i, last)], 0, col0 + jnp.where(i <= last, jf, nf - 1))
        return index_map

    def wd_map(i, jf, te, nu):
        last = nu[0] - 1
        return (j, te[jnp.minimum(i, last)], jnp.where(i <= last, jf, nf - 1), 0)

    return pl.pallas_call(
        functools.partial(_gmm_kernel, sub=sub),
        name="moe_expert_swiglu",
        grid_spec=pltpu.PrefetchScalarGridSpec(
            num_scalar_prefetch=2,
            grid=(rows // tg, nf),
            in_specs=[pl.BlockSpec((tg, d), row_map),
                      pl.BlockSpec((None, None, d, fc), w_map(0)),
                      pl.BlockSpec((None, None, d, fc), w_map(nf)),
                      pl.BlockSpec((None, None, fc, d), wd_map)],
            out_specs=pl.BlockSpec((tg, d), row_map),
            scratch_shapes=[pltpu.VMEM((tg, d), BF16), pltpu.VMEM((tg, d), F32)]),
        out_shape=jax.ShapeDtypeStruct((rows, d), F32),
        compiler_params=_params("arbitrary", "arbitrary"),
    )(tile_expert, n_used, xs, w_gu, w_gu, w_down)


def _combine_kernel(src_ref, rw_ref, x_ref, mod_ref, ng_ref, ys_ref, o_ref, ybuf, sems, *, tm):
    part = tm // COMBINE_PARTS
    groups = part // ROW_UNROLL

    def issue(g, carry, sem):
        for u in range(ROW_UNROLL):
            t = g * ROW_UNROLL + u
            for k in range(2):
                _row_copy(ys_ref, src_ref[2 * t + k], ybuf.at[k], t, sem).start()
        return carry

    def drain(g, carry, sem):
        for _ in range(2 * ROW_UNROLL):
            _row_copy(ys_ref, 0, ybuf.at[0], 0, sem).wait()
        return carry

    for pi in range(COMBINE_PARTS):
        lax.fori_loop(pi * groups, (pi + 1) * groups,
                      functools.partial(issue, sem=sems.at[pi]), 0)

    m = mod_ref[0, 0]
    ng = ng_ref[0]
    for pi in range(COMBINE_PARTS):
        lax.fori_loop(0, groups, functools.partial(drain, sem=sems.at[pi]), 0)
        rows = slice(pi * part, (pi + 1) * part)
        rw = rw_ref[0, rows, :]
        y = rw[:, 0:1] * ybuf[0, rows, :] + rw[:, 1:2] * ybuf[1, rows, :]
        o_ref[0, rows, :] = x_ref[0, rows, :] + _row(m, GATE2) * _rms(y, _row(ng, 3))


def _combine_layer(dest_flat, rw, x, mod, norm_g, ys, layer):
    b, s, d = x.shape
    tm = min(1024, s)
    nt = s // tm
    return pl.pallas_call(
        functools.partial(_combine_kernel, tm=tm),
        name="moe_combine",
        grid=(b, nt),
        in_specs=[pl.BlockSpec((2 * tm,), lambda i, t: (i * nt + t,), memory_space=pltpu.SMEM),
                  pl.BlockSpec((1, tm, LANES), lambda i, t: (i, t, 0)),
                  pl.BlockSpec((1, tm, d), lambda i, t: (i, t, 0)),
                  pl.BlockSpec((1, 1, N_MOD, d), lambda i, t: (layer, i, 0, 0)),
                  pl.BlockSpec((1, 4, d), lambda i, t: (layer, 0, 0)),
                  pl.BlockSpec(memory_space=pl.ANY)],
        out_specs=pl.BlockSpec((1, tm, d), lambda i, t: (i, t, 0)),
        scratch_shapes=[pltpu.VMEM((2, tm, d), F32), pltpu.SemaphoreType.DMA((COMBINE_PARTS,))],
        out_shape=jax.ShapeDtypeStruct(x.shape, F32),
        compiler_params=_params("arbitrary", "arbitrary"),
    )(dest_flat, rw, x, mod, norm_g, ys)


def _moe_layer(h2, ri, rw, counts, xmid, mod, norm_g, w_gu, w_down, layer, j):
    b, s, d = xmid.shape
    n = b * s
    tg = min(1024, n)
    cnt = counts[0, :N_EXPERTS].astype(jnp.int32)
    padded = (cnt + tg - 1) // tg * tg
    offs = jnp.concatenate([jnp.zeros((1,), jnp.int32), jnp.cumsum(padded)]).astype(jnp.int32)
    n_tiles = 2 * n // tg + N_EXPERTS
    starts = jnp.arange(n_tiles, dtype=jnp.int32) * tg
    tile_expert = jnp.minimum(jnp.sum(starts[:, None] >= offs[None, 1:], axis=1),
                              N_EXPERTS - 1).astype(jnp.int32)
    n_used = (offs[N_EXPERTS:] // tg).astype(jnp.int32)
    sel = ri[:, :, 0:2]
    rank = ri[:, :, 2:4]
    starts_of = jnp.sum(jnp.where(sel[..., None] == jnp.arange(N_EXPERTS, dtype=jnp.int32),
                                  offs[:N_EXPERTS], 0), axis=-1)
    dest_flat = (starts_of + rank).reshape(-1).astype(jnp.int32)

    xs = _scatter_rows(offs, dest_flat, h2.reshape(n, d), n_tiles * tg, tg)
    ys = _expert_ffn(tile_expert, n_used, xs, w_gu, w_down, tg, j)
    return _combine_layer(dest_flat, rw, xmid, mod, norm_g, ys, layer)


def kernel(x, c, w_mod, b_mod, norm_g, pool_w, pool_scale, w_qkv, w_o, subln_g, lambda_vecs,
           rel_bias, ffn_w_gu, ffn_w_down, w_router, moe_w_gu, moe_w_down):
    depth = w_mod.shape[0]
    s = x.shape[1]
    d = x.shape[2]
    assert d == 2 * N_HEADS * HEAD_DIM
    t_attn = min(256, s)

    mod = _modulation(c, w_mod, b_mod)
    pool_w16 = pool_w.astype(BF16)
    ffn_gu16 = ffn_w_gu.astype(BF16)
    ffn_down16 = ffn_w_down.astype(BF16)
    wk16 = w_qkv[:, :, d:2 * d].astype(BF16)
    wqvt16 = jnp.swapaxes(jnp.concatenate([w_qkv[:, :, :d], w_qkv[:, :, 2 * d:]], axis=2),
                          1, 2).astype(BF16)
    wo16 = w_o.astype(BF16)
    moe_gu16 = moe_w_gu.astype(BF16)
    moe_down16 = moe_w_down.astype(BF16)
    router_pad = jnp.pad(w_router, ((0, 0), (0, 0), (0, LANES - N_EXPERTS)))
    router_hi = router_pad.astype(BF16)
    router_lo = (router_pad - router_hi.astype(F32)).astype(BF16)
    router_split = jnp.stack([router_hi, router_lo], axis=1)
    bias = _bias_tiles(rel_bias, t_attn)

    for i in range(depth):
        j = i // 2
        if i % 2 == 0:
            x = _pool_layer(x, mod, norm_g, pool_w16, pool_scale, i, j)
            x = _ffn_layer(x, mod, norm_g, ffn_gu16, ffn_down16, i, j)
        else:
            lam_init = 0.8 - 0.6 * math.exp(-0.3 * i)
            k, qt, vt = _qkv_layer(x, mod, norm_g, wk16, wqvt16, i, j, t_attn)
            o = _attention(k, qt, vt, bias, lambda_vecs, subln_g, j, lam_init, t_attn)
            xmid, h2, ri, rw, counts = _wo_route_layer(o, x, mod, norm_g, wo16, router_split, i, j)
            x = _moe_layer(h2, ri, rw, counts, xmid, mod, norm_g, moe_gu16, moe_down16, i, j)
    return x
```

```python
import functools
import math

import jax
import jax.numpy as jnp
from jax import lax
from jax.experimental import pallas as pl
from jax.experimental.pallas import tpu as pltpu

F32 = jnp.float32
BF16 = jnp.bfloat16

EPS = 1e-6
N_MOD = 6
POOL_WINDOWS = (2, 4, 8, 16)
POOL_HALO = 32
N_HEADS = 8
HEAD_DIM = 64
V_HEAD_DIM = 2 * HEAD_DIM
N_BUCKETS = 32
MAX_DISTANCE = 128
N_EXPERTS = 8
LANES = 128
LOG2E = math.log2(math.e)
NEG_BIG = -1e30
MASKED_TILE = 2 * NEG_BIG
TILE_DIAG, TILE_SUB, TILE_FAR, TILE_MASKED = range(4)
N_TILE_KINDS = 4
ONES_ROWS = 16
TICK_PAIRS_PER_ITER = 3
VMEM_LIMIT = 56 * 1024 * 1024
TOKEN_TILE = 512
ROW_TILE = 1024
ATTN_TILE = 256
FF_CHUNK = 256
MOD_COLS = 1536

SHIFT1, SCALE1, GATE1, SHIFT2, SCALE2, GATE2 = range(N_MOD)


def _params(*sem):
    return pltpu.CompilerParams(dimension_semantics=sem, vmem_limit_bytes=VMEM_LIMIT)


def _rms(x, g):
    return x * lax.rsqrt(jnp.mean(x * x, axis=-1, keepdims=True) + EPS) * g


def _norm_mod(x, g, scale, shift):
    return _rms(x, g) * (1.0 + scale) + shift


def _silu(x):
    return x * (1.0 / (1.0 + jnp.exp(-x)))


def _row(a, i):
    return a[i:i + 1, :]


def _mod_kernel(c_ref, w_ref, b_ref, o_ref):
    c = _silu(c_ref[...])
    o_ref[0] = jnp.dot(c, w_ref[0], precision=lax.Precision.HIGHEST,
                       preferred_element_type=F32) + b_ref[0]


def _modulation(c, w_mod, b_mod):
    depth, d, e = w_mod.shape
    b = c.shape[0]
    nc = MOD_COLS
    out = pl.pallas_call(
        _mod_kernel,
        name="adaln_mod",
        grid=(depth, e // nc),
        in_specs=[pl.BlockSpec((b, d), lambda l, j: (0, 0)),
                  pl.BlockSpec((1, d, nc), lambda l, j: (l, 0, j)),
                  pl.BlockSpec((1, 1, nc), lambda l, j: (l, 0, j))],
        out_specs=pl.BlockSpec((1, b, nc), lambda l, j: (l, 0, j)),
        out_shape=jax.ShapeDtypeStruct((depth, b, e), F32),
        compiler_params=_params("arbitrary", "arbitrary"),
    )(c, w_mod, b_mod.reshape(depth, 1, e))
    return out.reshape(depth, b, N_MOD, d)


def _pool_kernel(x_ref, mod_ref, ng_ref, pw_ref, ps_ref, o_ref, hbuf, sum2, sum4, sum8, *, ts):
    s = pl.program_id(1)
    x = x_ref[0]
    m = mod_ref[0, 0]
    ng = ng_ref[0]
    d = x.shape[-1]
    cg = d // len(POOL_WINDOWS)
    n = POOL_HALO + ts
    h = _norm_mod(x, _row(ng, 0), _row(m, SCALE1), _row(m, SHIFT1))

    @pl.when(s == 0)
    def _():
        hbuf[0:POOL_HALO, :] = jnp.zeros((POOL_HALO, d), F32)

    @pl.when(s > 0)
    def _():
        hbuf[0:POOL_HALO, :] = hbuf[ts:n, :]

    hbuf[POOL_HALO:n, :] = h

    sum2[8:n, :] = hbuf[8:n, :] + hbuf[7:n - 1, :]
    sum4[16:n, :] = sum2[16:n, cg:] + sum2[14:n - 2, cg:]
    sum8[24:n, :] = sum4[24:n, cg:] + sum4[20:n - 4, cg:]
    sum16 = sum8[POOL_HALO:n, cg:] + sum8[POOL_HALO - 8:n - 8, cg:]
    sums = (sum2[POOL_HALO:n, 0:cg], sum4[POOL_HALO:n, 0:cg], sum8[POOL_HALO:n, 0:cg], sum16)

    count = s * ts + lax.broadcasted_iota(jnp.int32, (ts, 1), 0) + 1
    outs = []
    for g, w in enumerate(POOL_WINDOWS):
        pooled = sums[g] / jnp.minimum(count, w).astype(F32) - h[:, g * cg:(g + 1) * cg]
        outs.append(jnp.dot(pooled.astype(BF16), pw_ref[0, g], preferred_element_type=F32))
    y = jnp.concatenate(outs, axis=-1) * ps_ref[0]
    o_ref[0] = x + _row(m, GATE1) * _rms(y, _row(ng, 1))


def _pool_layer(x, mod, norm_g, pool_w, pool_scale, layer, j):
    b, s, d = x.shape
    ts = min(TOKEN_TILE, s)
    g, cg, _ = pool_w.shape[1:]
    assert POOL_WINDOWS == (2, 4, 8, 16) and g == len(POOL_WINDOWS)
    rows = POOL_HALO + ts
    return pl.pallas_call(
        functools.partial(_pool_kernel, ts=ts),
        name="pool_mixer",
        grid=(b, s // ts),
        in_specs=[pl.BlockSpec((1, ts, d), lambda i, t: (i, t, 0)),
                  pl.BlockSpec((1, 1, N_MOD, d), lambda i, t: (layer, i, 0, 0)),
                  pl.BlockSpec((1, 4, d), lambda i, t: (layer, 0, 0)),
                  pl.BlockSpec((1, g, cg, cg), lambda i, t: (j, 0, 0, 0)),
                  pl.BlockSpec((1, 1, d), lambda i, t: (j, 0, 0))],
        out_specs=pl.BlockSpec((1, ts, d), lambda i, t: (i, t, 0)),
        out_shape=jax.ShapeDtypeStruct(x.shape, F32),
        scratch_shapes=[pltpu.VMEM((rows, d), F32), pltpu.VMEM((rows, d), F32),
                        pltpu.VMEM((rows, d - cg), F32), pltpu.VMEM((rows, d - 2 * cg), F32)],
        compiler_params=_params("arbitrary", "arbitrary"),
    )(x, mod, norm_g, pool_w, pool_scale.reshape(-1, 1, d))


def _ffn_kernel(x_ref, mod_ref, ng_ref, wgu_ref, wd_ref, o_ref, *, f, fc):
    x = x_ref[0]
    m = mod_ref[0, 0]
    ng = ng_ref[0]
    h = _norm_mod(x, _row(ng, 2), _row(m, SCALE2), _row(m, SHIFT2)).astype(BF16)
    acc = jnp.zeros(x.shape, F32)
    for c in range(f // fc):
        gate = jnp.dot(h, wgu_ref[0, :, c * fc:(c + 1) * fc], preferred_element_type=F32)
        up = jnp.dot(h, wgu_ref[0, :, f + c * fc:f + (c + 1) * fc], preferred_element_type=F32)
        hm = (_silu(gate) * up).astype(BF16)
        acc = acc + jnp.dot(hm, wd_ref[0, c * fc:(c + 1) * fc, :], preferred_element_type=F32)
    o_ref[0] = x + _row(m, GATE2) * _rms(acc, _row(ng, 3))


def _ffn_layer(x, mod, norm_g, w_gu, w_down, layer, j):
    b, s, d = x.shape
    f = w_down.shape[1]
    tm = min(TOKEN_TILE, s)
    fc = FF_CHUNK if f % FF_CHUNK == 0 else f
    return pl.pallas_call(
        functools.partial(_ffn_kernel, f=f, fc=fc),
        name="dense_swiglu",
        grid=(b, s // tm),
        in_specs=[pl.BlockSpec((1, tm, d), lambda i, t: (i, t, 0)),
                  pl.BlockSpec((1, 1, N_MOD, d), lambda i, t: (layer, i, 0, 0)),
                  pl.BlockSpec((1, 4, d), lambda i, t: (layer, 0, 0)),
                  pl.BlockSpec((1, d, 2 * f), lambda i, t: (j, 0, 0)),
                  pl.BlockSpec((1, f, d), lambda i, t: (j, 0, 0))],
        out_specs=pl.BlockSpec((1, tm, d), lambda i, t: (i, t, 0)),
        out_shape=jax.ShapeDtypeStruct(x.shape, F32),
        compiler_params=_params("arbitrary", "arbitrary"),
    )(x, mod, norm_g, w_gu, w_down)


def _qkv_kernel(x_ref, mod_ref, ng_ref, wk_ref, wqvt_ref, k_ref, qt_ref, vt_ref, *, t):
    x = x_ref[0]
    m = mod_ref[0, 0]
    ng = ng_ref[0]
    d = x.shape[-1]
    h = _norm_mod(x, _row(ng, 0), _row(m, SCALE1), _row(m, SHIFT1)).astype(BF16)
    k_ref[0] = jnp.dot(h, wk_ref[0], preferred_element_type=F32).astype(BF16)
    qvt = lax.dot_general(wqvt_ref[0], h, (((1,), (1,)), ((), ())), preferred_element_type=F32)
    for kk in range(x.shape[0] // t):
        cols = slice(kk * t, (kk + 1) * t)
        qt_ref[0, kk] = (qvt[:d, cols] * (HEAD_DIM ** -0.5 * LOG2E)).astype(BF16)
        vt_ref[0, kk] = qvt[d:, cols].astype(BF16)


def _qkv_layer(x, mod, norm_g, w_k, w_qvt, layer, j, t):
    b, s, d = x.shape
    tm = min(TOKEN_TILE, s)
    transposed = jax.ShapeDtypeStruct((b, s // t, d, t), BF16)
    transposed_spec = pl.BlockSpec((1, tm // t, d, t), lambda i, u: (i, u, 0, 0))
    return pl.pallas_call(
        functools.partial(_qkv_kernel, t=t),
        name="qkv_proj",
        grid=(b, s // tm),
        in_specs=[pl.BlockSpec((1, tm, d), lambda i, u: (i, u, 0)),
                  pl.BlockSpec((1, 1, N_MOD, d), lambda i, u: (layer, i, 0, 0)),
                  pl.BlockSpec((1, 4, d), lambda i, u: (layer, 0, 0)),
                  pl.BlockSpec((1, d, d), lambda i, u: (j, 0, 0)),
                  pl.BlockSpec((1, 2 * d, d), lambda i, u: (j, 0, 0))],
        out_specs=[pl.BlockSpec((1, tm, d), lambda i, u: (i, u, 0)),
                   transposed_spec, transposed_spec],
        out_shape=[jax.ShapeDtypeStruct((b, s, d), BF16), transposed, transposed],
        compiler_params=_params("arbitrary", "arbitrary"),
    )(x, mod, norm_g, w_k, w_qvt)


def _bias_kernel(rb_ref, o_ref, *, t):
    h = pl.program_id(0)
    kind = pl.program_id(1)
    key = lax.broadcasted_iota(jnp.int32, (t, t), 0)
    qry = lax.broadcasted_iota(jnp.int32, (t, t), 1)
    n_signed = kind * t + qry - key
    n = jnp.maximum(n_signed, 0)
    max_exact = N_BUCKETS // 2
    nf = jnp.maximum(n, 1).astype(F32)
    large = max_exact + (jnp.log(nf / max_exact) / math.log(MAX_DISTANCE / max_exact)
                         * (N_BUCKETS - max_exact)).astype(jnp.int32)
    large = jnp.minimum(large, N_BUCKETS - 1)
    bucket = jnp.where(n < max_exact, n, large)
    bias = jnp.zeros((t, t), F32)
    for b in range(N_BUCKETS):
        bias = jnp.where(bucket == b, rb_ref[b * N_HEADS + h], bias)
    bias = jnp.where(n_signed < 0, NEG_BIG, bias * LOG2E)
    o_ref[0, 0] = jnp.where(kind == TILE_MASKED, MASKED_TILE, bias)


def _bias_tiles(rel_bias, t):
    return pl.pallas_call(
        functools.partial(_bias_kernel, t=t),
        name="t5_bias_tiles",
        grid=(N_HEADS, N_TILE_KINDS),
        in_specs=[pl.BlockSpec(memory_space=pltpu.SMEM)],
        out_specs=pl.BlockSpec((1, 1, t, t), lambda h, kind: (h, kind, 0, 0)),
        out_shape=jax.ShapeDtypeStruct((N_HEADS, N_TILE_KINDS, t, t), F32),
        compiler_params=_params("arbitrary", "arbitrary"),
    )(rel_bias.reshape(-1))


def _attn_kernel(qt_ref, k_ref, vt_ref, bias_ref, lv_ref, sg_ref, o_ref,
                 qm, s0, s1, p0, p1, al0, al1, m_sc, acc_sc, *, t, nq, lam_init):
    m_sc[...] = jnp.full(m_sc.shape, NEG_BIG, F32)
    acc_sc[...] = jnp.zeros(acc_sc.shape, F32)
    s1[...] = jnp.full(s1.shape, MASKED_TILE, F32)
    p0[...] = jnp.zeros(p0.shape, BF16)
    al0[...] = jnp.ones(al0.shape, F32)
    ones = jnp.ones((ONES_ROWS, t), BF16)
    q_all = qt_ref[0]
    feat = lax.broadcasted_iota(jnp.int32, q_all.shape, 1)
    zero = jnp.zeros_like(q_all)
    qm[0] = jnp.where(feat < HEAD_DIM, q_all, zero)
    qm[1] = jnp.where(feat >= HEAD_DIM, q_all, zero)
    last = nq - 1

    def advance(pair):
        ki, qi = pair
        wrap = qi == last
        return jnp.where(wrap, ki + 1, ki), jnp.where(wrap, ki + 1, qi + 1)

    def clipped(pair):
        return jnp.minimum(pair[0], last), jnp.minimum(pair[1], last)

    def scores(pair, s_out):
        kl, ql = clipped(pair)
        kind = jnp.where(pair[0] > last, TILE_MASKED, jnp.minimum(ql - kl, TILE_FAR))
        kt = k_ref[0, pl.ds(pl.multiple_of(kl * t, t), t), :]
        bias = bias_ref[0, kind]
        for c in range(2):
            s_out[c] = jnp.dot(kt, qm[c, ql], preferred_element_type=F32) + bias

    def softmax(pair, s_in, p_out, al_out):
        _, ql = clipped(pair)
        for c in range(2):
            sc = s_in[c]
            m_old = m_sc[ql, c]
            m_new = jnp.maximum(m_old, jnp.max(sc, axis=0, keepdims=True))
            p_out[c] = jnp.exp2(sc - m_new).astype(BF16)
            al_out[c] = jnp.exp2(m_old - m_new)
            m_sc[ql, c] = m_new

    def values(pair, p_in, al_in):
        kl, ql = clipped(pair)
        va = jnp.concatenate([vt_ref[0, kl], ones], axis=0)
        for c in range(2):
            acc_sc[ql, c] = (al_in[c] * acc_sc[ql, c]
                             + jnp.dot(va, p_in[c], preferred_element_type=F32))

    def tick_pair(_, carry):
        cur, lag1, lag2 = carry
        scores(cur, s0)
        softmax(lag1, s1, p1, al1)
        values(lag2, p0, al0)
        nxt = advance(cur)
        scores(nxt, s1)
        softmax(cur, s0, p0, al0)
        values(lag1, p1, al1)
        return advance(nxt), nxt, cur

    def tick_group(i, carry):
        for _ in range(TICK_PAIRS_PER_ITER):
            carry = tick_pair(i, carry)
        return carry

    n_ticks = nq * (nq + 1) // 2 + 2
    per_iter = 2 * TICK_PAIRS_PER_ITER
    start = (jnp.int32(0), jnp.int32(0))
    lax.fori_loop(0, (n_ticks + per_iter - 1) // per_iter, tick_group, (start, start, start))

    lv = lv_ref[0]
    lam = (jnp.exp(jnp.sum(_row(lv, 0) * _row(lv, 1), axis=-1, keepdims=True))
           - jnp.exp(jnp.sum(_row(lv, 2) * _row(lv, 3), axis=-1, keepdims=True)) + lam_init)
    gain = sg_ref[0] * (1.0 - lam_init)

    def finish(qi, carry):
        a0 = acc_sc[qi, 0]
        a1 = acc_sc[qi, 1]
        ot = (a0[:V_HEAD_DIM] / a0[V_HEAD_DIM:V_HEAD_DIM + 1]
              - lam * (a1[:V_HEAD_DIM] / a1[V_HEAD_DIM:V_HEAD_DIM + 1]))
        ot = ot * lax.rsqrt(jnp.mean(ot * ot, axis=0, keepdims=True) + EPS)
        o_ref[0, pl.ds(pl.multiple_of(qi * t, t), t), :] = (ot.T * gain).astype(BF16)
        return carry

    lax.fori_loop(0, nq, finish, 0)


def _attention(k, qt, vt, bias, lambda_vecs, subln_g, j, lam_init, t):
    b, s, d = k.shape
    nq = s // t
    assert t >= MAX_DISTANCE and s % t == 0
    score_buf = pltpu.VMEM((2, t, t), F32)
    prob_buf = pltpu.VMEM((2, t, t), BF16)
    alpha_buf = pltpu.VMEM((2, 1, t), F32)
    head_cols = pl.BlockSpec((1, s, V_HEAD_DIM), lambda i, h: (i, 0, h))
    head_rows = pl.BlockSpec((1, nq, V_HEAD_DIM, t), lambda i, h: (i, 0, h, 0))
    return pl.pallas_call(
        functools.partial(_attn_kernel, t=t, nq=nq, lam_init=lam_init),
        name="diff_attention",
        grid=(b, N_HEADS),
        in_specs=[head_rows, head_cols, head_rows,
                  pl.BlockSpec((1, N_TILE_KINDS, t, t), lambda i, h: (h, 0, 0, 0)),
                  pl.BlockSpec((1, 4, HEAD_DIM), lambda i, h: (j, 0, 0)),
                  pl.BlockSpec((1, 1, V_HEAD_DIM), lambda i, h: (j, 0, 0))],
        out_specs=head_cols,
        out_shape=jax.ShapeDtypeStruct((b, s, d), BF16),
        scratch_shapes=[pltpu.VMEM((2, nq, V_HEAD_DIM, t), BF16),
                        score_buf, score_buf, prob_buf, prob_buf, alpha_buf, alpha_buf,
                        pltpu.VMEM((nq, 2, 1, t), F32),
                        pltpu.VMEM((nq, 2, V_HEAD_DIM + ONES_ROWS, t), F32)],
        compiler_params=_params("arbitrary", "arbitrary"),
    )(qt, k, vt, bias, lambda_vecs, subln_g.reshape(-1, 1, V_HEAD_DIM))


def _wo_route_kernel(o_ref, x_ref, mod_ref, ng_ref, wo_ref, wr_ref,
                     xmid_ref, h2_ref, ri_ref, rw_ref, cnt_ref, run_sc, *, tm):
    first = jnp.logical_and(pl.program_id(0) == 0, pl.program_id(1) == 0)

    @pl.when(first)
    def _():
        run_sc[...] = jnp.zeros(run_sc.shape, F32)

    x = x_ref[0]
    m = mod_ref[0, 0]
    ng = ng_ref[0]
    y = jnp.dot(o_ref[0], wo_ref[0], preferred_element_type=F32)
    xn = x + _row(m, GATE1) * _rms(y, _row(ng, 1))
    xmid_ref[0] = xn
    h2 = _norm_mod(xn, _row(ng, 2), _row(m, SCALE2), _row(m, SHIFT2))
    h2_ref[0] = h2

    h_hi = h2.astype(BF16)
    h_lo = (h2 - h_hi.astype(F32)).astype(BF16)
    logits = (jnp.dot(h_hi, wr_ref[0, 0], preferred_element_type=F32)
              + jnp.dot(h_lo, wr_ref[0, 0], preferred_element_type=F32)
              + jnp.dot(h_hi, wr_ref[0, 1], preferred_element_type=F32))
    lane = lax.broadcasted_iota(jnp.int32, logits.shape, 1)
    lane_f = lane.astype(F32)
    neg_inf = jnp.float32(-jnp.inf)
    logits = jnp.where(lane < N_EXPERTS, logits, neg_inf)
    m1 = jnp.max(logits, axis=-1, keepdims=True)
    i1 = jnp.min(jnp.where(logits == m1, lane_f, float(LANES)), axis=-1, keepdims=True)
    oh1 = lane_f == i1
    rest = jnp.where(oh1, neg_inf, logits)
    m2 = jnp.max(rest, axis=-1, keepdims=True)
    i2 = jnp.min(jnp.where(rest == m2, lane_f, float(LANES)), axis=-1, keepdims=True)
    oh2 = lane_f == i2
    i1 = i1.astype(jnp.int32)
    i2 = i2.astype(jnp.int32)
    e21 = jnp.exp(m2 - m1)
    w1 = 1.0 / (1.0 + e21)
    w2 = e21 * w1

    oh = jnp.where(jnp.logical_or(oh1, oh2), 1.0, 0.0)
    r = lax.broadcasted_iota(jnp.int32, (tm, tm), 0)
    c = lax.broadcasted_iota(jnp.int32, (tm, tm), 1)
    tri = jnp.where(r > c, 1.0, 0.0).astype(BF16)
    prefix = jnp.dot(tri, oh.astype(BF16), preferred_element_type=F32) + run_sc[...]
    r1 = jnp.sum(jnp.where(oh1, prefix, 0.0), axis=-1, keepdims=True).astype(jnp.int32)
    r2 = jnp.sum(jnp.where(oh2, prefix, 0.0), axis=-1, keepdims=True).astype(jnp.int32)
    run_sc[...] = run_sc[...] + jnp.sum(oh, axis=0, keepdims=True)

    ri_ref[0] = jnp.where(lane == 0, i1, jnp.where(lane == 1, i2,
                          jnp.where(lane == 2, r1, jnp.where(lane == 3, r2, 0))))
    rw_ref[0] = jnp.where(lane == 0, w1, jnp.where(lane == 1, w2, 0.0))
    cnt_ref[...] = run_sc[...]


def _wo_route_layer(o, x, mod, norm_g, w_o, w_router, layer, j):
    b, s, d = x.shape
    tm = min(TOKEN_TILE, s)
    act = jax.ShapeDtypeStruct((b, s, d), F32)
    tile = pl.BlockSpec((1, tm, d), lambda i, t: (i, t, 0))
    lanes = pl.BlockSpec((1, tm, LANES), lambda i, t: (i, t, 0))
    return pl.pallas_call(
        functools.partial(_wo_route_kernel, tm=tm),
        name="attn_out_router",
        grid=(b, s // tm),
        in_specs=[tile, tile,
                  pl.BlockSpec((1, 1, N_MOD, d), lambda i, t: (layer, i, 0, 0)),
                  pl.BlockSpec((1, 4, d), lambda i, t: (layer, 0, 0)),
                  pl.BlockSpec((1, d, d), lambda i, t: (j, 0, 0)),
                  pl.BlockSpec((1, 2, d, LANES), lambda i, t: (j, 0, 0, 0))],
        out_specs=[tile, tile, lanes, lanes, pl.BlockSpec((1, LANES), lambda i, t: (0, 0))],
        out_shape=[act, act,
                   jax.ShapeDtypeStruct((b, s, LANES), jnp.int32),
                   jax.ShapeDtypeStruct((b, s, LANES), F32),
                   jax.ShapeDtypeStruct((1, LANES), F32)],
        scratch_shapes=[pltpu.VMEM((1, LANES), F32)],
        compiler_params=_params("arbitrary", "arbitrary"),
    )(o, x, mod, norm_g, w_o, w_router)


def _row_copy(src, src_row, dst, dst_row, sem):
    return pltpu.make_async_copy(src.at[pl.ds(src_row, 1)], dst.at[pl.ds(dst_row, 1)], sem)


ROW_UNROLL = 8
COMBINE_PARTS = 4


def _scatter_kernel(offs_ref, dst_ref, h_ref, xs_ref, zbuf, sem, *, tm, tg):
    @pl.when(pl.program_id(0) == 0)
    def _():
        zbuf[...] = jnp.zeros(zbuf.shape, F32)
        for e in range(N_EXPERTS):
            @pl.when(offs_ref[e + 1] > offs_ref[e])
            def _():
                start = pl.multiple_of(offs_ref[e + 1] - tg, tg)
                cp = pltpu.make_async_copy(zbuf, xs_ref.at[pl.ds(start, tg)], sem)
                cp.start()
                cp.wait()

    def issue(g, carry):
        for u in range(ROW_UNROLL):
            t = g * ROW_UNROLL + u
            for k in range(2):
                _row_copy(h_ref, t, xs_ref, dst_ref[2 * t + k], sem).start()
        return carry

    lax.fori_loop(0, tm // ROW_UNROLL, issue, 0)

    def drain(g, carry):
        for _ in range(2 * ROW_UNROLL):
            _row_copy(h_ref, 0, xs_ref, 0, sem).wait()
        return carry

    lax.fori_loop(0, tm // ROW_UNROLL, drain, 0)


def _scatter_rows(offs, dest_flat, h2, rows, tg):
    n, d = h2.shape
    tm = min(ROW_TILE, n)
    return pl.pallas_call(
        functools.partial(_scatter_kernel, tm=tm, tg=tg),
        name="moe_group_rows",
        grid_spec=pltpu.PrefetchScalarGridSpec(
            num_scalar_prefetch=1,
            grid=(n // tm,),
            in_specs=[pl.BlockSpec((2 * tm,), lambda i, offs: (i,), memory_space=pltpu.SMEM),
                      pl.BlockSpec((tm, d), lambda i, offs: (i, 0))],
            out_specs=pl.BlockSpec(memory_space=pl.ANY),
            scratch_shapes=[pltpu.VMEM((tg, d), F32), pltpu.SemaphoreType.DMA(())]),
        out_shape=jax.ShapeDtypeStruct((rows, d), F32),
        compiler_params=_params("arbitrary"),
    )(offs, dest_flat, h2)


def _gmm_kernel(te_ref, nu_ref, x_ref, wg_ref, wu_ref, wd_ref, o_ref, xb, acc, *, sub):
    i = pl.program_id(0)
    jf = pl.program_id(1)

    @pl.when(i < nu_ref[0])
    def _():
        @pl.when(jf == 0)
        def _():
            xb[...] = x_ref[...].astype(BF16)
            acc[...] = jnp.zeros(acc.shape, F32)

        x = xb[...]
        for c in range(wd_ref.shape[0] // sub):
            cols = slice(c * sub, (c + 1) * sub)
            gate = jnp.dot(x, wg_ref[:, cols], preferred_element_type=F32)
            up = jnp.dot(x, wu_ref[:, cols], preferred_element_type=F32)
            hm = (_silu(gate) * up).astype(BF16)
            acc[...] += jnp.dot(hm, wd_ref[cols, :], preferred_element_type=F32)

        @pl.when(jf == pl.num_programs(1) - 1)
        def _():
            o_ref[...] = acc[...]


def _expert_ffn(tile_expert, n_used, xs, w_gu, w_down, tg, j):
    rows, d = xs.shape
    f = w_down.shape[2]
    sub = FF_CHUNK if f % FF_CHUNK == 0 else f
    fc = f // 2 if f % (2 * sub) == 0 else f
    nf = f // fc

    def row_map(i, jf, te, nu):
        return (jnp.minimum(i, nu[0] - 1), 0)

    def w_map(col0):
        def index_map(i, jf, te, nu):
            last = nu[0] - 1
            return (j, te[jnp.minimum(i, last)], 0, col0 + jnp.where(i <= last, jf, nf - 1))
        return index_map

    def wd_map(i, jf, te, nu):
        last = nu[0] - 1
        return (j, te[jnp.minimum(i, last)], jnp.where(i <= last, jf, nf - 1), 0)

    return pl.pallas_call(
        functools.partial(_gmm_kernel, sub=sub),
        name="moe_expert_swiglu",
        grid_spec=pltpu.PrefetchScalarGridSpec(
            num_scalar_prefetch=2,
            grid=(rows // tg, nf),
            in_specs=[pl.BlockSpec((tg, d), row_map),
                      pl.BlockSpec((None, None, d, fc), w_map(0)),
                      pl.BlockSpec((None, None, d, fc), w_map(nf)),
                      pl.BlockSpec((None, None, fc, d), wd_map)],
            out_specs=pl.BlockSpec((tg, d), row_map),
            scratch_shapes=[pltpu.VMEM((tg, d), BF16), pltpu.VMEM((tg, d), F32)]),
        out_shape=jax.ShapeDtypeStruct((rows, d), F32),
        compiler_params=_params("arbitrary", "arbitrary"),
    )(tile_expert, n_used, xs, w_gu, w_gu, w_down)


def _combine_kernel(src_ref, rw_ref, x_ref, mod_ref, ng_ref, ys_ref, o_ref, ybuf, sems, *, tm):
    part = tm // COMBINE_PARTS
    groups = part // ROW_UNROLL

    def issue(g, carry, sem):
        for u in range(ROW_UNROLL):
            t = g * ROW_UNROLL + u
            for k in range(2):
                _row_copy(ys_ref, src_ref[2 * t + k], ybuf.at[k], t, sem).start()
        return carry

    def drain(g, carry, sem):
        for _ in range(2 * ROW_UNROLL):
            _row_copy(ys_ref, 0, ybuf.at[0], 0, sem).wait()
        return carry

    for pi in range(COMBINE_PARTS):
        lax.fori_loop(pi * groups, (pi + 1) * groups,
                      functools.partial(issue, sem=sems.at[pi]), 0)

    m = mod_ref[0, 0]
    ng = ng_ref[0]
    for pi in range(COMBINE_PARTS):
        lax.fori_loop(0, groups, functools.partial(drain, sem=sems.at[pi]), 0)
        rows = slice(pi * part, (pi + 1) * part)
        rw = rw_ref[0, rows, :]
        y = rw[:, 0:1] * ybuf[0, rows, :] + rw[:, 1:2] * ybuf[1, rows, :]
        o_ref[0, rows, :] = x_ref[0, rows, :] + _row(m, GATE2) * _rms(y, _row(ng, 3))


def _combine_layer(dest_flat, rw, x, mod, norm_g, ys, layer):
    b, s, d = x.shape
    tm = min(ROW_TILE, s)
    nt = s // tm
    return pl.pallas_call(
        functools.partial(_combine_kernel, tm=tm),
        name="moe_combine",
        grid=(b, nt),
        in_specs=[pl.BlockSpec((2 * tm,), lambda i, t: (i * nt + t,), memory_space=pltpu.SMEM),
                  pl.BlockSpec((1, tm, LANES), lambda i, t: (i, t, 0)),
                  pl.BlockSpec((1, tm, d), lambda i, t: (i, t, 0)),
                  pl.BlockSpec((1, 1, N_MOD, d), lambda i, t: (layer, i, 0, 0)),
                  pl.BlockSpec((1, 4, d), lambda i, t: (layer, 0, 0)),
                  pl.BlockSpec(memory_space=pl.ANY)],
        out_specs=pl.BlockSpec((1, tm, d), lambda i, t: (i, t, 0)),
        scratch_shapes=[pltpu.VMEM((2, tm, d), F32), pltpu.SemaphoreType.DMA((COMBINE_PARTS,))],
        out_shape=jax.ShapeDtypeStruct(x.shape, F32),
        compiler_params=_params("arbitrary", "arbitrary"),
    )(dest_flat, rw, x, mod, norm_g, ys)


def _moe_layer(h2, ri, rw, counts, xmid, mod, norm_g, w_gu, w_down, layer, j):
    b, s, d = xmid.shape
    n = b * s
    tg = min(ROW_TILE, n)
    cnt = counts[0, :N_EXPERTS].astype(jnp.int32)
    padded = (cnt + tg - 1) // tg * tg
    offs = jnp.concatenate([jnp.zeros((1,), jnp.int32), jnp.cumsum(padded)]).astype(jnp.int32)
    n_tiles = 2 * n // tg + N_EXPERTS
    starts = jnp.arange(n_tiles, dtype=jnp.int32) * tg
    tile_expert = jnp.minimum(jnp.sum(starts[:, None] >= offs[None, 1:], axis=1),
                              N_EXPERTS - 1).astype(jnp.int32)
    n_used = (offs[N_EXPERTS:] // tg).astype(jnp.int32)
    sel = ri[:, :, 0:2]
    rank = ri[:, :, 2:4]
    starts_of = jnp.sum(jnp.where(sel[..., None] == jnp.arange(N_EXPERTS, dtype=jnp.int32),
                                  offs[:N_EXPERTS], 0), axis=-1)
    dest_flat = (starts_of + rank).reshape(-1).astype(jnp.int32)

    xs = _scatter_rows(offs, dest_flat, h2.reshape(n, d), n_tiles * tg, tg)
    ys = _expert_ffn(tile_expert, n_used, xs, w_gu, w_down, tg, j)
    return _combine_layer(dest_flat, rw, xmid, mod, norm_g, ys, layer)


def kernel(x, c, w_mod, b_mod, norm_g, pool_w, pool_scale, w_qkv, w_o, subln_g, lambda_vecs,
           rel_bias, ffn_w_gu, ffn_w_down, w_router, moe_w_gu, moe_w_down):
    depth = w_mod.shape[0]
    s = x.shape[1]
    d = x.shape[2]
    assert d == 2 * N_HEADS * HEAD_DIM
    t_attn = min(ATTN_TILE, s)

    mod = _modulation(c, w_mod, b_mod)
    pool_w16 = pool_w.astype(BF16)
    ffn_gu16 = ffn_w_gu.astype(BF16)
    ffn_down16 = ffn_w_down.astype(BF16)
    wk16 = w_qkv[:, :, d:2 * d].astype(BF16)
    wqvt16 = jnp.swapaxes(jnp.concatenate([w_qkv[:, :, :d], w_qkv[:, :, 2 * d:]], axis=2),
                          1, 2).astype(BF16)
    wo16 = w_o.astype(BF16)
    moe_gu16 = moe_w_gu.astype(BF16)
    moe_down16 = moe_w_down.astype(BF16)
    router_pad = jnp.pad(w_router, ((0, 0), (0, 0), (0, LANES - N_EXPERTS)))
    router_hi = router_pad.astype(BF16)
    router_lo = (router_pad - router_hi.astype(F32)).astype(BF16)
    router_split = jnp.stack([router_hi, router_lo], axis=1)
    bias = _bias_tiles(rel_bias, t_attn)

    for i in range(depth):
        j = i // 2
        if i % 2 == 0:
            x = _pool_layer(x, mod, norm_g, pool_w16, pool_scale, i, j)
            x = _ffn_layer(x, mod, norm_g, ffn_gu16, ffn_down16, i, j)
        else:
            lam_init = 0.8 - 0.6 * math.exp(-0.3 * i)
            k, qt, vt = _qkv_layer(x, mod, norm_g, wk16, wqvt16, i, j, t_attn)
            o = _attention(k, qt, vt, bias, lambda_vecs, subln_g, j, lam_init, t_attn)
            xmid, h2, ri, rw, counts = _wo_route_layer(o, x, mod, norm_g, wo16, router_split, i, j)
            x = _moe_layer(h2, ri, rw, counts, xmid, mod, norm_g, moe_gu16, moe_down16, i, j)
    return x
```

```python
import functools
import math

import jax
import jax.numpy as jnp
from jax import lax
from jax.experimental import pallas as pl
from jax.experimental.pallas import tpu as pltpu

F32 = jnp.float32
BF16 = jnp.bfloat16

EPS = 1e-6
N_MOD = 6
POOL_WINDOWS = (2, 4, 8, 16)
POOL_HALO = 32
N_HEADS = 8
HEAD_DIM = 64
V_HEAD_DIM = 2 * HEAD_DIM
N_BUCKETS = 32
MAX_DISTANCE = 128
N_EXPERTS = 8
TOP_K = 2
LANES = 128
LOG2E = math.log2(math.e)
NEG_BIG = -1e30
MASKED_TILE = 2 * NEG_BIG
TILE_DIAG, TILE_SUB, TILE_FAR, TILE_MASKED = range(4)
N_TILE_KINDS = 4
ONES_ROWS = 16
TICK_PAIRS_PER_ITER = 3
VMEM_LIMIT = 56 * 1024 * 1024
TOKEN_TILE = 512
ROW_TILE = 1024
ATTN_TILE = 256
FF_CHUNK = 256
MOD_COLS = 1536

SHIFT1, SCALE1, GATE1, SHIFT2, SCALE2, GATE2 = range(N_MOD)


def _params(*sem):
    return pltpu.CompilerParams(dimension_semantics=sem, vmem_limit_bytes=VMEM_LIMIT)


def _rms(x, g):
    return x * lax.rsqrt(jnp.mean(x * x, axis=-1, keepdims=True) + EPS) * g


def _norm_mod(x, g, scale, shift):
    return _rms(x, g) * (1.0 + scale) + shift


def _silu(x):
    return x * (1.0 / (1.0 + jnp.exp(-x)))


def _row(a, i):
    return a[i:i + 1, :]


def _mod_kernel(c_ref, w_ref, b_ref, o_ref):
    c = _silu(c_ref[...])
    o_ref[0] = jnp.dot(c, w_ref[0], precision=lax.Precision.HIGHEST,
                       preferred_element_type=F32) + b_ref[0]


def _modulation(c, w_mod, b_mod):
    depth, d, e = w_mod.shape
    b = c.shape[0]
    nc = MOD_COLS
    out = pl.pallas_call(
        _mod_kernel,
        name="adaln_mod",
        grid=(depth, e // nc),
        in_specs=[pl.BlockSpec((b, d), lambda l, j: (0, 0)),
                  pl.BlockSpec((1, d, nc), lambda l, j: (l, 0, j)),
                  pl.BlockSpec((1, 1, nc), lambda l, j: (l, 0, j))],
        out_specs=pl.BlockSpec((1, b, nc), lambda l, j: (l, 0, j)),
        out_shape=jax.ShapeDtypeStruct((depth, b, e), F32),
        compiler_params=_params("arbitrary", "arbitrary"),
    )(c, w_mod, b_mod.reshape(depth, 1, e))
    return out.reshape(depth, b, N_MOD, d)


def _pool_kernel(x_ref, mod_ref, ng_ref, pw_ref, ps_ref, o_ref, hbuf, sum2, sum4, sum8, *, ts):
    s = pl.program_id(1)
    x = x_ref[0]
    m = mod_ref[0, 0]
    ng = ng_ref[0]
    d = x.shape[-1]
    cg = d // len(POOL_WINDOWS)
    n = POOL_HALO + ts
    h = _norm_mod(x, _row(ng, 0), _row(m, SCALE1), _row(m, SHIFT1))

    @pl.when(s == 0)
    def _():
        hbuf[0:POOL_HALO, :] = jnp.zeros((POOL_HALO, d), F32)

    @pl.when(s > 0)
    def _():
        hbuf[0:POOL_HALO, :] = hbuf[ts:n, :]

    hbuf[POOL_HALO:n, :] = h

    sum2[8:n, :] = hbuf[8:n, :] + hbuf[7:n - 1, :]
    sum4[16:n, :] = sum2[16:n, cg:] + sum2[14:n - 2, cg:]
    sum8[24:n, :] = sum4[24:n, cg:] + sum4[20:n - 4, cg:]
    sum16 = sum8[POOL_HALO:n, cg:] + sum8[POOL_HALO - 8:n - 8, cg:]
    sums = (sum2[POOL_HALO:n, 0:cg], sum4[POOL_HALO:n, 0:cg], sum8[POOL_HALO:n, 0:cg], sum16)

    count = s * ts + lax.broadcasted_iota(jnp.int32, (ts, 1), 0) + 1
    outs = []
    for g, w in enumerate(POOL_WINDOWS):
        pooled = sums[g] / jnp.minimum(count, w).astype(F32) - h[:, g * cg:(g + 1) * cg]
        outs.append(jnp.dot(pooled.astype(BF16), pw_ref[0, g], preferred_element_type=F32))
    y = jnp.concatenate(outs, axis=-1) * ps_ref[0]
    o_ref[0] = x + _row(m, GATE1) * _rms(y, _row(ng, 1))


def _pool_layer(x, mod, norm_g, pool_w, pool_scale, layer, j):
    b, s, d = x.shape
    ts = min(TOKEN_TILE, s)
    g, cg, _ = pool_w.shape[1:]
    assert POOL_WINDOWS == (2, 4, 8, 16) and g == len(POOL_WINDOWS)
    rows = POOL_HALO + ts
    return pl.pallas_call(
        functools.partial(_pool_kernel, ts=ts),
        name="pool_mixer",
        grid=(b, s // ts),
        in_specs=[pl.BlockSpec((1, ts, d), lambda i, t: (i, t, 0)),
                  pl.BlockSpec((1, 1, N_MOD, d), lambda i, t: (layer, i, 0, 0)),
                  pl.BlockSpec((1, 4, d), lambda i, t: (layer, 0, 0)),
                  pl.BlockSpec((1, g, cg, cg), lambda i, t: (j, 0, 0, 0)),
                  pl.BlockSpec((1, 1, d), lambda i, t: (j, 0, 0))],
        out_specs=pl.BlockSpec((1, ts, d), lambda i, t: (i, t, 0)),
        out_shape=jax.ShapeDtypeStruct(x.shape, F32),
        scratch_shapes=[pltpu.VMEM((rows, d), F32), pltpu.VMEM((rows, d), F32),
                        pltpu.VMEM((rows, d - cg), F32), pltpu.VMEM((rows, d - 2 * cg), F32)],
        compiler_params=_params("arbitrary", "arbitrary"),
    )(x, mod, norm_g, pool_w, pool_scale.reshape(-1, 1, d))


def _ffn_kernel(x_ref, mod_ref, ng_ref, wgu_ref, wd_ref, o_ref, *, f, fc):
    x = x_ref[0]
    m = mod_ref[0, 0]
    ng = ng_ref[0]
    h = _norm_mod(x, _row(ng, 2), _row(m, SCALE2), _row(m, SHIFT2)).astype(BF16)
    acc = jnp.zeros(x.shape, F32)
    for c in range(f // fc):
        gate = jnp.dot(h, wgu_ref[0, :, c * fc:(c + 1) * fc], preferred_element_type=F32)
        up = jnp.dot(h, wgu_ref[0, :, f + c * fc:f + (c + 1) * fc], preferred_element_type=F32)
        hm = (_silu(gate) * up).astype(BF16)
        acc = acc + jnp.dot(hm, wd_ref[0, c * fc:(c + 1) * fc, :], preferred_element_type=F32)
    o_ref[0] = x + _row(m, GATE2) * _rms(acc, _row(ng, 3))


def _ffn_layer(x, mod, norm_g, w_gu, w_down, layer, j):
    b, s, d = x.shape
    f = w_down.shape[1]
    tm = min(TOKEN_TILE, s)
    fc = FF_CHUNK if f % FF_CHUNK == 0 else f
    return pl.pallas_call(
        functools.partial(_ffn_kernel, f=f, fc=fc),
        name="dense_swiglu",
        grid=(b, s // tm),
        in_specs=[pl.BlockSpec((1, tm, d), lambda i, t: (i, t, 0)),
                  pl.BlockSpec((1, 1, N_MOD, d), lambda i, t: (layer, i, 0, 0)),
                  pl.BlockSpec((1, 4, d), lambda i, t: (layer, 0, 0)),
                  pl.BlockSpec((1, d, 2 * f), lambda i, t: (j, 0, 0)),
                  pl.BlockSpec((1, f, d), lambda i, t: (j, 0, 0))],
        out_specs=pl.BlockSpec((1, tm, d), lambda i, t: (i, t, 0)),
        out_shape=jax.ShapeDtypeStruct(x.shape, F32),
        compiler_params=_params("arbitrary", "arbitrary"),
    )(x, mod, norm_g, w_gu, w_down)


def _qkv_kernel(x_ref, mod_ref, ng_ref, wk_ref, wqvt_ref, k_ref, qt_ref, vt_ref, *, t):
    x = x_ref[0]
    m = mod_ref[0, 0]
    ng = ng_ref[0]
    d = x.shape[-1]
    h = _norm_mod(x, _row(ng, 0), _row(m, SCALE1), _row(m, SHIFT1)).astype(BF16)
    k_ref[0] = jnp.dot(h, wk_ref[0], preferred_element_type=F32).astype(BF16)
    qvt = lax.dot_general(wqvt_ref[0], h, (((1,), (1,)), ((), ())), preferred_element_type=F32)
    for kk in range(x.shape[0] // t):
        cols = slice(kk * t, (kk + 1) * t)
        qt_ref[0, kk] = (qvt[:d, cols] * (HEAD_DIM ** -0.5 * LOG2E)).astype(BF16)
        vt_ref[0, kk] = qvt[d:, cols].astype(BF16)


def _qkv_layer(x, mod, norm_g, w_k, w_qvt, layer, j, t):
    b, s, d = x.shape
    tm = min(TOKEN_TILE, s)
    transposed = jax.ShapeDtypeStruct((b, s // t, d, t), BF16)
    transposed_spec = pl.BlockSpec((1, tm // t, d, t), lambda i, u: (i, u, 0, 0))
    return pl.pallas_call(
        functools.partial(_qkv_kernel, t=t),
        name="qkv_proj",
        grid=(b, s // tm),
        in_specs=[pl.BlockSpec((1, tm, d), lambda i, u: (i, u, 0)),
                  pl.BlockSpec((1, 1, N_MOD, d), lambda i, u: (layer, i, 0, 0)),
                  pl.BlockSpec((1, 4, d), lambda i, u: (layer, 0, 0)),
                  pl.BlockSpec((1, d, d), lambda i, u: (j, 0, 0)),
                  pl.BlockSpec((1, 2 * d, d), lambda i, u: (j, 0, 0))],
        out_specs=[pl.BlockSpec((1, tm, d), lambda i, u: (i, u, 0)),
                   transposed_spec, transposed_spec],
        out_shape=[jax.ShapeDtypeStruct((b, s, d), BF16), transposed, transposed],
        compiler_params=_params("arbitrary", "arbitrary"),
    )(x, mod, norm_g, w_k, w_qvt)


def _bias_kernel(rb_ref, o_ref, *, t):
    h = pl.program_id(0)
    kind = pl.program_id(1)
    key = lax.broadcasted_iota(jnp.int32, (t, t), 0)
    qry = lax.broadcasted_iota(jnp.int32, (t, t), 1)
    n_signed = kind * t + qry - key
    n = jnp.maximum(n_signed, 0)
    max_exact = N_BUCKETS // 2
    nf = jnp.maximum(n, 1).astype(F32)
    large = max_exact + (jnp.log(nf / max_exact) / math.log(MAX_DISTANCE / max_exact)
                         * (N_BUCKETS - max_exact)).astype(jnp.int32)
    large = jnp.minimum(large, N_BUCKETS - 1)
    bucket = jnp.where(n < max_exact, n, large)
    bias = jnp.zeros((t, t), F32)
    for b in range(N_BUCKETS):
        bias = jnp.where(bucket == b, rb_ref[b * N_HEADS + h], bias)
    bias = jnp.where(n_signed < 0, NEG_BIG, bias * LOG2E)
    o_ref[0, 0] = jnp.where(kind == TILE_MASKED, MASKED_TILE, bias)


def _bias_tiles(rel_bias, t):
    return pl.pallas_call(
        functools.partial(_bias_kernel, t=t),
        name="t5_bias_tiles",
        grid=(N_HEADS, N_TILE_KINDS),
        in_specs=[pl.BlockSpec(memory_space=pltpu.SMEM)],
        out_specs=pl.BlockSpec((1, 1, t, t), lambda h, kind: (h, kind, 0, 0)),
        out_shape=jax.ShapeDtypeStruct((N_HEADS, N_TILE_KINDS, t, t), F32),
        compiler_params=_params("arbitrary", "arbitrary"),
    )(rel_bias.reshape(-1))


def _attn_kernel(qt_ref, k_ref, vt_ref, bias_ref, lv_ref, sg_ref, o_ref,
                 qm, s0, s1, p0, p1, al0, al1, m_sc, acc_sc, *, t, nq, lam_init):
    m_sc[...] = jnp.full(m_sc.shape, NEG_BIG, F32)
    acc_sc[...] = jnp.zeros(acc_sc.shape, F32)
    s1[...] = jnp.full(s1.shape, MASKED_TILE, F32)
    p0[...] = jnp.zeros(p0.shape, BF16)
    al0[...] = jnp.ones(al0.shape, F32)
    ones = jnp.ones((ONES_ROWS, t), BF16)
    q_all = qt_ref[0]
    feat = lax.broadcasted_iota(jnp.int32, q_all.shape, 1)
    zero = jnp.zeros_like(q_all)
    qm[0] = jnp.where(feat < HEAD_DIM, q_all, zero)
    qm[1] = jnp.where(feat >= HEAD_DIM, q_all, zero)
    last = nq - 1

    def advance(pair):
        ki, qi = pair
        wrap = qi == last
        return jnp.where(wrap, ki + 1, ki), jnp.where(wrap, ki + 1, qi + 1)

    def clipped(pair):
        return jnp.minimum(pair[0], last), jnp.minimum(pair[1], last)

    def scores(pair, s_out):
        kl, ql = clipped(pair)
        kind = jnp.where(pair[0] > last, TILE_MASKED, jnp.minimum(ql - kl, TILE_FAR))
        kt = k_ref[0, pl.ds(pl.multiple_of(kl * t, t), t), :]
        bias = bias_ref[0, kind]
        for c in range(2):
            s_out[c] = jnp.dot(kt, qm[c, ql], preferred_element_type=F32) + bias

    def softmax(pair, s_in, p_out, al_out):
        _, ql = clipped(pair)
        for c in range(2):
            sc = s_in[c]
            m_old = m_sc[ql, c]
            m_new = jnp.maximum(m_old, jnp.max(sc, axis=0, keepdims=True))
            p_out[c] = jnp.exp2(sc - m_new).astype(BF16)
            al_out[c] = jnp.exp2(m_old - m_new)
            m_sc[ql, c] = m_new

    def values(pair, p_in, al_in):
        kl, ql = clipped(pair)
        va = jnp.concatenate([vt_ref[0, kl], ones], axis=0)
        for c in range(2):
            acc_sc[ql, c] = (al_in[c] * acc_sc[ql, c]
                             + jnp.dot(va, p_in[c], preferred_element_type=F32))

    def tick_pair(_, carry):
        cur, lag1, lag2 = carry
        scores(cur, s0)
        softmax(lag1, s1, p1, al1)
        values(lag2, p0, al0)
        nxt = advance(cur)
        scores(nxt, s1)
        softmax(cur, s0, p0, al0)
        values(lag1, p1, al1)
        return advance(nxt), nxt, cur

    def tick_group(i, carry):
        for _ in range(TICK_PAIRS_PER_ITER):
            carry = tick_pair(i, carry)
        return carry

    n_ticks = nq * (nq + 1) // 2 + 2
    per_iter = 2 * TICK_PAIRS_PER_ITER
    start = (jnp.int32(0), jnp.int32(0))
    lax.fori_loop(0, (n_ticks + per_iter - 1) // per_iter, tick_group, (start, start, start))

    lv = lv_ref[0]
    lam = (jnp.exp(jnp.sum(_row(lv, 0) * _row(lv, 1), axis=-1, keepdims=True))
           - jnp.exp(jnp.sum(_row(lv, 2) * _row(lv, 3), axis=-1, keepdims=True)) + lam_init)
    gain = sg_ref[0] * (1.0 - lam_init)

    def finish(qi, carry):
        a0 = acc_sc[qi, 0]
        a1 = acc_sc[qi, 1]
        ot = (a0[:V_HEAD_DIM] / a0[V_HEAD_DIM:V_HEAD_DIM + 1]
              - lam * (a1[:V_HEAD_DIM] / a1[V_HEAD_DIM:V_HEAD_DIM + 1]))
        ot = ot * lax.rsqrt(jnp.mean(ot * ot, axis=0, keepdims=True) + EPS)
        o_ref[0, pl.ds(pl.multiple_of(qi * t, t), t), :] = (ot.T * gain).astype(BF16)
        return carry

    lax.fori_loop(0, nq, finish, 0)


def _attention(k, qt, vt, bias, lambda_vecs, subln_g, j, lam_init, t):
    b, s, d = k.shape
    nq = s // t
    assert t >= MAX_DISTANCE and s % t == 0
    score_buf = pltpu.VMEM((2, t, t), F32)
    prob_buf = pltpu.VMEM((2, t, t), BF16)
    alpha_buf = pltpu.VMEM((2, 1, t), F32)
    head_cols = pl.BlockSpec((1, s, V_HEAD_DIM), lambda i, h: (i, 0, h))
    head_rows = pl.BlockSpec((1, nq, V_HEAD_DIM, t), lambda i, h: (i, 0, h, 0))
    return pl.pallas_call(
        functools.partial(_attn_kernel, t=t, nq=nq, lam_init=lam_init),
        name="diff_attention",
        grid=(b, N_HEADS),
        in_specs=[head_rows, head_cols, head_rows,
                  pl.BlockSpec((1, N_TILE_KINDS, t, t), lambda i, h: (h, 0, 0, 0)),
                  pl.BlockSpec((1, 4, HEAD_DIM), lambda i, h: (j, 0, 0)),
                  pl.BlockSpec((1, 1, V_HEAD_DIM), lambda i, h: (j, 0, 0))],
        out_specs=head_cols,
        out_shape=jax.ShapeDtypeStruct((b, s, d), BF16),
        scratch_shapes=[pltpu.VMEM((2, nq, V_HEAD_DIM, t), BF16),
                        score_buf, score_buf, prob_buf, prob_buf, alpha_buf, alpha_buf,
                        pltpu.VMEM((nq, 2, 1, t), F32),
                        pltpu.VMEM((nq, 2, V_HEAD_DIM + ONES_ROWS, t), F32)],
        compiler_params=_params("arbitrary", "arbitrary"),
    )(qt, k, vt, bias, lambda_vecs, subln_g.reshape(-1, 1, V_HEAD_DIM))


def _wo_route_kernel(o_ref, x_ref, mod_ref, ng_ref, wo_ref, wr_ref,
                     xmid_ref, h2_ref, ri_ref, rw_ref, cnt_ref, run_sc, *, tm):
    first = jnp.logical_and(pl.program_id(0) == 0, pl.program_id(1) == 0)

    @pl.when(first)
    def _():
        run_sc[...] = jnp.zeros(run_sc.shape, F32)

    x = x_ref[0]
    m = mod_ref[0, 0]
    ng = ng_ref[0]
    y = jnp.dot(o_ref[0], wo_ref[0], preferred_element_type=F32)
    xn = x + _row(m, GATE1) * _rms(y, _row(ng, 1))
    xmid_ref[0] = xn
    h2 = _norm_mod(xn, _row(ng, 2), _row(m, SCALE2), _row(m, SHIFT2))
    h2_ref[0] = h2

    h_hi = h2.astype(BF16)
    h_lo = (h2 - h_hi.astype(F32)).astype(BF16)
    hi_both = jnp.dot(h_hi, wr_ref[0], preferred_element_type=F32)
    logits = (hi_both[:, :LANES] + hi_both[:, LANES:]
              + jnp.dot(h_lo, wr_ref[0, :, :LANES], preferred_element_type=F32))
    lane = lax.broadcasted_iota(jnp.int32, logits.shape, 1)
    lane_f = lane.astype(F32)
    neg_inf = jnp.float32(-jnp.inf)
    logits = jnp.where(lane < N_EXPERTS, logits, neg_inf)
    m1 = jnp.max(logits, axis=-1, keepdims=True)
    i1 = jnp.min(jnp.where(logits == m1, lane_f, float(LANES)), axis=-1, keepdims=True)
    oh1 = lane_f == i1
    rest = jnp.where(oh1, neg_inf, logits)
    m2 = jnp.max(rest, axis=-1, keepdims=True)
    i2 = jnp.min(jnp.where(rest == m2, lane_f, float(LANES)), axis=-1, keepdims=True)
    oh2 = lane_f == i2
    i1 = i1.astype(jnp.int32)
    i2 = i2.astype(jnp.int32)
    e21 = jnp.exp(m2 - m1)
    w1 = 1.0 / (1.0 + e21)
    w2 = e21 * w1

    oh = jnp.where(jnp.logical_or(oh1, oh2), 1.0, 0.0)
    r = lax.broadcasted_iota(jnp.int32, (tm, tm), 0)
    c = lax.broadcasted_iota(jnp.int32, (tm, tm), 1)
    tri = jnp.where(r > c, 1.0, 0.0).astype(BF16)
    prefix = jnp.dot(tri, oh.astype(BF16), preferred_element_type=F32) + run_sc[...]
    r1 = jnp.sum(jnp.where(oh1, prefix, 0.0), axis=-1, keepdims=True).astype(jnp.int32)
    r2 = jnp.sum(jnp.where(oh2, prefix, 0.0), axis=-1, keepdims=True).astype(jnp.int32)
    run_sc[...] = run_sc[...] + jnp.sum(oh, axis=0, keepdims=True)

    ri_ref[0] = jnp.where(lane == 0, i1, jnp.where(lane == 1, i2,
                          jnp.where(lane == 2, r1, jnp.where(lane == 3, r2, 0))))
    rw_ref[0] = jnp.where(lane == 0, w1, jnp.where(lane == 1, w2, 0.0))
    cnt_ref[...] = run_sc[...]


def _wo_route_layer(o, x, mod, norm_g, w_o, w_router, layer, j):
    b, s, d = x.shape
    tm = min(TOKEN_TILE, s)
    act = jax.ShapeDtypeStruct((b, s, d), F32)
    tile = pl.BlockSpec((1, tm, d), lambda i, t: (i, t, 0))
    lanes = pl.BlockSpec((1, tm, LANES), lambda i, t: (i, t, 0))
    return pl.pallas_call(
        functools.partial(_wo_route_kernel, tm=tm),
        name="attn_out_router",
        grid=(b, s // tm),
        in_specs=[tile, tile,
                  pl.BlockSpec((1, 1, N_MOD, d), lambda i, t: (layer, i, 0, 0)),
                  pl.BlockSpec((1, 4, d), lambda i, t: (layer, 0, 0)),
                  pl.BlockSpec((1, d, d), lambda i, t: (j, 0, 0)),
                  pl.BlockSpec((1, d, 2 * LANES), lambda i, t: (j, 0, 0))],
        out_specs=[tile, tile, lanes, lanes, pl.BlockSpec((1, LANES), lambda i, t: (0, 0))],
        out_shape=[act, act,
                   jax.ShapeDtypeStruct((b, s, LANES), jnp.int32),
                   jax.ShapeDtypeStruct((b, s, LANES), F32),
                   jax.ShapeDtypeStruct((1, LANES), F32)],
        scratch_shapes=[pltpu.VMEM((1, LANES), F32)],
        compiler_params=_params("arbitrary", "arbitrary"),
    )(o, x, mod, norm_g, w_o, w_router)


def _row_copy(src, src_row, dst, dst_row, sem):
    return pltpu.make_async_copy(src.at[pl.ds(src_row, 1)], dst.at[pl.ds(dst_row, 1)], sem)


ROW_UNROLL = 8
COMBINE_PARTS = 4


def _scatter_kernel(offs_ref, dst_ref, h_ref, xs_ref, zbuf, sem, *, tm, tg):
    @pl.when(pl.program_id(0) == 0)
    def _():
        zbuf[...] = jnp.zeros(zbuf.shape, F32)
        for e in range(N_EXPERTS):
            @pl.when(offs_ref[e + 1] > offs_ref[e])
            def _():
                start = pl.multiple_of(offs_ref[e + 1] - tg, tg)
                cp = pltpu.make_async_copy(zbuf, xs_ref.at[pl.ds(start, tg)], sem)
                cp.start()
                cp.wait()

    def issue(g, carry):
        for u in range(ROW_UNROLL):
            t = g * ROW_UNROLL + u
            for k in range(TOP_K):
                _row_copy(h_ref, t, xs_ref, dst_ref[TOP_K * t + k], sem).start()
        return carry

    lax.fori_loop(0, tm // ROW_UNROLL, issue, 0)

    def drain(g, carry):
        for _ in range(TOP_K * ROW_UNROLL):
            _row_copy(h_ref, 0, xs_ref, 0, sem).wait()
        return carry

    lax.fori_loop(0, tm // ROW_UNROLL, drain, 0)


def _scatter_rows(offs, dest_flat, h2, rows, tg):
    n, d = h2.shape
    tm = min(ROW_TILE, n)
    return pl.pallas_call(
        functools.partial(_scatter_kernel, tm=tm, tg=tg),
        name="moe_group_rows",
        grid_spec=pltpu.PrefetchScalarGridSpec(
            num_scalar_prefetch=1,
            grid=(n // tm,),
            in_specs=[pl.BlockSpec((TOP_K * tm,), lambda i, offs: (i,), memory_space=pltpu.SMEM),
                      pl.BlockSpec((tm, d), lambda i, offs: (i, 0))],
            out_specs=pl.BlockSpec(memory_space=pl.ANY),
            scratch_shapes=[pltpu.VMEM((tg, d), F32), pltpu.SemaphoreType.DMA(())]),
        out_shape=jax.ShapeDtypeStruct((rows, d), F32),
        compiler_params=_params("arbitrary"),
    )(offs, dest_flat, h2)


def _gmm_kernel(te_ref, nu_ref, x_ref, wg_ref, wu_ref, wd_ref, o_ref, xb, acc, *, sub):
    i = pl.program_id(0)
    jf = pl.program_id(1)

    @pl.when(i < nu_ref[0])
    def _():
        @pl.when(jf == 0)
        def _():
            xb[...] = x_ref[...].astype(BF16)
            acc[...] = jnp.zeros(acc.shape, F32)

        x = xb[...]
        for c in range(wd_ref.shape[0] // sub):
            cols = slice(c * sub, (c + 1) * sub)
            gate = jnp.dot(x, wg_ref[:, cols], preferred_element_type=F32)
            up = jnp.dot(x, wu_ref[:, cols], preferred_element_type=F32)
            hm = (_silu(gate) * up).astype(BF16)
            acc[...] += jnp.dot(hm, wd_ref[cols, :], preferred_element_type=F32)

        @pl.when(jf == pl.num_programs(1) - 1)
        def _():
            o_ref[...] = acc[...]


def _expert_ffn(tile_expert, n_used, xs, w_gu, w_down, tg, j):
    rows, d = xs.shape
    f = w_down.shape[2]
    sub = FF_CHUNK if f % FF_CHUNK == 0 else f
    fc = f // 2 if f % (2 * sub) == 0 else f
    nf = f // fc

    def row_map(i, jf, te, nu):
        return (jnp.minimum(i, nu[0] - 1), 0)

    def w_map(col0):
        def index_map(i, jf, te, nu):
            last = nu[0] - 1
            return (j, te[jnp.minimum(i, last)], 0, col0 + jnp.where(i <= last, jf, nf - 1))
        return index_map

    def wd_map(i, jf, te, nu):
        last = nu[0] - 1
        return (j, te[jnp.minimum(i, last)], jnp.where(i <= last, jf, nf - 1), 0)

    return pl.pallas_call(
        functools.partial(_gmm_kernel, sub=sub),
        name="moe_expert_swiglu",
        grid_spec=pltpu.PrefetchScalarGridSpec(
            num_scalar_prefetch=2,
            grid=(rows // tg, nf),
            in_specs=[pl.BlockSpec((tg, d), row_map),
                      pl.BlockSpec((None, None, d, fc), w_map(0)),
                      pl.BlockSpec((None, None, d, fc), w_map(nf)),
                      pl.BlockSpec((None, None, fc, d), wd_map)],
            out_specs=pl.BlockSpec((tg, d), row_map),
            scratch_shapes=[pltpu.VMEM((tg, d), BF16), pltpu.VMEM((tg, d), F32)]),
        out_shape=jax.ShapeDtypeStruct((rows, d), F32),
        compiler_params=_params("arbitrary", "arbitrary"),
    )(tile_expert, n_used, xs, w_gu, w_gu, w_down)


def _combine_kernel(src_ref, rw_ref, x_ref, mod_ref, ng_ref, ys_ref, o_ref, ybuf, sems, *, tm):
    part = tm // COMBINE_PARTS
    groups = part // ROW_UNROLL

    def issue(g, carry, sem):
        for u in range(ROW_UNROLL):
            t = g * ROW_UNROLL + u
            for k in range(TOP_K):
                _row_copy(ys_ref, src_ref[TOP_K * t + k], ybuf.at[k], t, sem).start()
        return carry

    def drain(g, carry, sem):
        for _ in range(TOP_K * ROW_UNROLL):
            _row_copy(ys_ref, 0, ybuf.at[0], 0, sem).wait()
        return carry

    for pi in range(COMBINE_PARTS):
        lax.fori_loop(pi * groups, (pi + 1) * groups,
                      functools.partial(issue, sem=sems.at[pi]), 0)

    m = mod_ref[0, 0]
    ng = ng_ref[0]
    for pi in range(COMBINE_PARTS):
        lax.fori_loop(0, groups, functools.partial(drain, sem=sems.at[pi]), 0)
        rows = slice(pi * part, (pi + 1) * part)
        rw = rw_ref[0, rows, :]
        y = rw[:, 0:1] * ybuf[0, rows, :] + rw[:, 1:2] * ybuf[1, rows, :]
        o_ref[0, rows, :] = x_ref[0, rows, :] + _row(m, GATE2) * _rms(y, _row(ng, 3))


def _combine_layer(dest_flat, rw, x, mod, norm_g, ys, layer):
    b, s, d = x.shape
    tm = min(ROW_TILE, s)
    nt = s // tm
    return pl.pallas_call(
        functools.partial(_combine_kernel, tm=tm),
        name="moe_combine",
        grid=(b, nt),
        in_specs=[pl.BlockSpec((TOP_K * tm,), lambda i, t: (i * nt + t,), memory_space=pltpu.SMEM),
                  pl.BlockSpec((1, tm, LANES), lambda i, t: (i, t, 0)),
                  pl.BlockSpec((1, tm, d), lambda i, t: (i, t, 0)),
                  pl.BlockSpec((1, 1, N_MOD, d), lambda i, t: (layer, i, 0, 0)),
                  pl.BlockSpec((1, 4, d), lambda i, t: (layer, 0, 0)),
                  pl.BlockSpec(memory_space=pl.ANY)],
        out_specs=pl.BlockSpec((1, tm, d), lambda i, t: (i, t, 0)),
        scratch_shapes=[pltpu.VMEM((TOP_K, tm, d), F32),
                        pltpu.SemaphoreType.DMA((COMBINE_PARTS,))],
        out_shape=jax.ShapeDtypeStruct(x.shape, F32),
        compiler_params=_params("arbitrary", "arbitrary"),
    )(dest_flat, rw, x, mod, norm_g, ys)


def _moe_layer(h2, ri, rw, counts, xmid, mod, norm_g, w_gu, w_down, layer, j):
    b, s, d = xmid.shape
    n = b * s
    tg = min(ROW_TILE, n)
    cnt = counts[0, :N_EXPERTS].astype(jnp.int32)
    padded = (cnt + tg - 1) // tg * tg
    offs = jnp.concatenate([jnp.zeros((1,), jnp.int32), jnp.cumsum(padded)]).astype(jnp.int32)
    n_tiles = 2 * n // tg + N_EXPERTS
    starts = jnp.arange(n_tiles, dtype=jnp.int32) * tg
    tile_expert = jnp.minimum(jnp.sum(starts[:, None] >= offs[None, 1:], axis=1),
                              N_EXPERTS - 1).astype(jnp.int32)
    n_used = (offs[N_EXPERTS:] // tg).astype(jnp.int32)
    sel = ri[:, :, 0:2]
    rank = ri[:, :, 2:4]
    starts_of = jnp.sum(jnp.where(sel[..., None] == jnp.arange(N_EXPERTS, dtype=jnp.int32),
                                  offs[:N_EXPERTS], 0), axis=-1)
    dest_flat = (starts_of + rank).reshape(-1).astype(jnp.int32)

    xs = _scatter_rows(offs, dest_flat, h2.reshape(n, d), n_tiles * tg, tg)
    ys = _expert_ffn(tile_expert, n_used, xs, w_gu, w_down, tg, j)
    return _combine_layer(dest_flat, rw, xmid, mod, norm_g, ys, layer)


def kernel(x, c, w_mod, b_mod, norm_g, pool_w, pool_scale, w_qkv, w_o, subln_g, lambda_vecs,
           rel_bias, ffn_w_gu, ffn_w_down, w_router, moe_w_gu, moe_w_down):
    depth = w_mod.shape[0]
    s = x.shape[1]
    d = x.shape[2]
    assert d == 2 * N_HEADS * HEAD_DIM
    t_attn = min(ATTN_TILE, s)

    mod = _modulation(c, w_mod, b_mod)
    pool_w16 = pool_w.astype(BF16)
    ffn_gu16 = ffn_w_gu.astype(BF16)
    ffn_down16 = ffn_w_down.astype(BF16)
    wk16 = w_qkv[:, :, d:2 * d].astype(BF16)
    wqvt16 = jnp.swapaxes(jnp.concatenate([w_qkv[:, :, :d], w_qkv[:, :, 2 * d:]], axis=2),
                          1, 2).astype(BF16)
    wo16 = w_o.astype(BF16)
    moe_gu16 = moe_w_gu.astype(BF16)
    moe_down16 = moe_w_down.astype(BF16)
    router_pad = jnp.pad(w_router, ((0, 0), (0, 0), (0, LANES - N_EXPERTS)))
    router_hi = router_pad.astype(BF16)
    router_lo = (router_pad - router_hi.astype(F32)).astype(BF16)
    router_split = jnp.concatenate([router_hi, router_lo], axis=2)
    bias = _bias_tiles(rel_bias, t_attn)

    for i in range(depth):
        j = i // 2
        if i % 2 == 0:
            x = _pool_layer(x, mod, norm_g, pool_w16, pool_scale, i, j)
            x = _ffn_layer(x, mod, norm_g, ffn_gu16, ffn_down16, i, j)
        else:
            lam_init = 0.8 - 0.6 * math.exp(-0.3 * i)
            k, qt, vt = _qkv_layer(x, mod, norm_g, wk16, wqvt16, i, j, t_attn)
            o = _attention(k, qt, vt, bias, lambda_vecs, subln_g, j, lam_init, t_attn)
            xmid, h2, ri, rw, counts = _wo_route_layer(o, x, mod, norm_g, wo16, router_split, i, j)
            x = _moe_layer(h2, ri, rw, counts, xmid, mod, norm_g, moe_gu16, moe_down16, i, j)
    return x
```

```python
import functools
import math

import jax
import jax.numpy as jnp
from jax import lax
from jax.experimental import pallas as pl
from jax.experimental.pallas import tpu as pltpu

F32 = jnp.float32
BF16 = jnp.bfloat16

EPS = 1e-6
N_MOD = 6
POOL_WINDOWS = (2, 4, 8, 16)
POOL_HALO = 32
N_HEADS = 8
HEAD_DIM = 64
V_HEAD_DIM = 2 * HEAD_DIM
N_BUCKETS = 32
MAX_DISTANCE = 128
N_EXPERTS = 8
TOP_K = 2
LANES = 128
LOG2E = math.log2(math.e)
NEG_BIG = -1e30
MASKED_TILE = 2 * NEG_BIG
TILE_DIAG, TILE_SUB, TILE_FAR, TILE_MASKED = range(4)
N_TILE_KINDS = 4
ONES_ROWS = 16
TICK_PAIRS_PER_ITER = 3
VMEM_LIMIT = 56 * 1024 * 1024
TOKEN_TILE = 512
ROW_TILE = 1024
ATTN_TILE = 256
FF_CHUNK = 256
MOD_COLS = 1536

SHIFT1, SCALE1, GATE1, SHIFT2, SCALE2, GATE2 = range(N_MOD)


def _params(*sem):
    return pltpu.CompilerParams(dimension_semantics=sem, vmem_limit_bytes=VMEM_LIMIT)


def _rms(x, g):
    return x * lax.rsqrt(jnp.mean(x * x, axis=-1, keepdims=True) + EPS) * g


def _norm_mod(x, g, scale, shift):
    return _rms(x, g) * (1.0 + scale) + shift


def _silu(x):
    return x * (1.0 / (1.0 + jnp.exp(-x)))


def _row(a, i):
    return a[i:i + 1, :]


def _mod_kernel(c_ref, w_ref, b_ref, o_ref):
    c = _silu(c_ref[...])
    o_ref[0] = jnp.dot(c, w_ref[0], precision=lax.Precision.HIGHEST,
                       preferred_element_type=F32) + b_ref[0]


def _modulation(c, w_mod, b_mod):
    depth, d, e = w_mod.shape
    b = c.shape[0]
    nc = MOD_COLS
    out = pl.pallas_call(
        _mod_kernel,
        name="adaln_mod",
        grid=(depth, e // nc),
        in_specs=[pl.BlockSpec((b, d), lambda l, j: (0, 0)),
                  pl.BlockSpec((1, d, nc), lambda l, j: (l, 0, j)),
                  pl.BlockSpec((1, 1, nc), lambda l, j: (l, 0, j))],
        out_specs=pl.BlockSpec((1, b, nc), lambda l, j: (l, 0, j)),
        out_shape=jax.ShapeDtypeStruct((depth, b, e), F32),
        compiler_params=_params("arbitrary", "arbitrary"),
    )(c, w_mod, b_mod.reshape(depth, 1, e))
    return out.reshape(depth, b, N_MOD, d)


def _pool_kernel(x_ref, mod_ref, ng_ref, pw_ref, ps_ref, o_ref, hbuf, sum2, sum4, sum8, *, ts):
    s = pl.program_id(1)
    x = x_ref[0]
    m = mod_ref[0, 0]
    ng = ng_ref[0]
    d = x.shape[-1]
    cg = d // len(POOL_WINDOWS)
    n = POOL_HALO + ts
    h = _norm_mod(x, _row(ng, 0), _row(m, SCALE1), _row(m, SHIFT1))

    @pl.when(s == 0)
    def _():
        hbuf[0:POOL_HALO, :] = jnp.zeros((POOL_HALO, d), F32)

    @pl.when(s > 0)
    def _():
        hbuf[0:POOL_HALO, :] = hbuf[ts:n, :]

    hbuf[POOL_HALO:n, :] = h

    sum2[8:n, :] = hbuf[8:n, :] + hbuf[7:n - 1, :]
    sum4[16:n, :] = sum2[16:n, cg:] + sum2[14:n - 2, cg:]
    sum8[24:n, :] = sum4[24:n, cg:] + sum4[20:n - 4, cg:]
    sum16 = sum8[POOL_HALO:n, cg:] + sum8[POOL_HALO - 8:n - 8, cg:]
    sums = (sum2[POOL_HALO:n, 0:cg], sum4[POOL_HALO:n, 0:cg], sum8[POOL_HALO:n, 0:cg], sum16)

    count = s * ts + lax.broadcasted_iota(jnp.int32, (ts, 1), 0) + 1
    outs = []
    for g, w in enumerate(POOL_WINDOWS):
        pooled = sums[g] / jnp.minimum(count, w).astype(F32) - h[:, g * cg:(g + 1) * cg]
        outs.append(jnp.dot(pooled.astype(BF16), pw_ref[0, g], preferred_element_type=F32))
    y = jnp.concatenate(outs, axis=-1) * ps_ref[0]
    o_ref[0] = x + _row(m, GATE1) * _rms(y, _row(ng, 1))


def _pool_layer(x, mod, norm_g, pool_w, pool_scale, layer, j):
    b, s, d = x.shape
    ts = min(TOKEN_TILE, s)
    g, cg, _ = pool_w.shape[1:]
    assert POOL_WINDOWS == (2, 4, 8, 16) and g == len(POOL_WINDOWS)
    rows = POOL_HALO + ts
    return pl.pallas_call(
        functools.partial(_pool_kernel, ts=ts),
        name="pool_mixer",
        grid=(b, s // ts),
        in_specs=[pl.BlockSpec((1, ts, d), lambda i, t: (i, t, 0)),
                  pl.BlockSpec((1, 1, N_MOD, d), lambda i, t: (layer, i, 0, 0)),
                  pl.BlockSpec((1, 4, d), lambda i, t: (layer, 0, 0)),
                  pl.BlockSpec((1, g, cg, cg), lambda i, t: (j, 0, 0, 0)),
                  pl.BlockSpec((1, 1, d), lambda i, t: (j, 0, 0))],
        out_specs=pl.BlockSpec((1, ts, d), lambda i, t: (i, t, 0)),
        out_shape=jax.ShapeDtypeStruct(x.shape, F32),
        scratch_shapes=[pltpu.VMEM((rows, d), F32), pltpu.VMEM((rows, d), F32),
                        pltpu.VMEM((rows, d - cg), F32), pltpu.VMEM((rows, d - 2 * cg), F32)],
        compiler_params=_params("arbitrary", "arbitrary"),
    )(x, mod, norm_g, pool_w, pool_scale.reshape(-1, 1, d))


def _ffn_kernel(x_ref, mod_ref, ng_ref, wgu_ref, wd_ref, o_ref, *, f, fc):
    x = x_ref[0]
    m = mod_ref[0, 0]
    ng = ng_ref[0]
    h = _norm_mod(x, _row(ng, 2), _row(m, SCALE2), _row(m, SHIFT2)).astype(BF16)
    acc = jnp.zeros(x.shape, F32)
    for c in range(f // fc):
        gate = jnp.dot(h, wgu_ref[0, :, c * fc:(c + 1) * fc], preferred_element_type=F32)
        up = jnp.dot(h, wgu_ref[0, :, f + c * fc:f + (c + 1) * fc], preferred_element_type=F32)
        hm = (_silu(gate) * up).astype(BF16)
        acc = acc + jnp.dot(hm, wd_ref[0, c * fc:(c + 1) * fc, :], preferred_element_type=F32)
    o_ref[0] = x + _row(m, GATE2) * _rms(acc, _row(ng, 3))


def _ffn_layer(x, mod, norm_g, w_gu, w_down, layer, j):
    b, s, d = x.shape
    f = w_down.shape[1]
    tm = min(TOKEN_TILE, s)
    fc = FF_CHUNK if f % FF_CHUNK == 0 else f
    return pl.pallas_call(
        functools.partial(_ffn_kernel, f=f, fc=fc),
        name="dense_swiglu",
        grid=(b, s // tm),
        in_specs=[pl.BlockSpec((1, tm, d), lambda i, t: (i, t, 0)),
                  pl.BlockSpec((1, 1, N_MOD, d), lambda i, t: (layer, i, 0, 0)),
                  pl.BlockSpec((1, 4, d), lambda i, t: (layer, 0, 0)),
                  pl.BlockSpec((1, d, 2 * f), lambda i, t: (j, 0, 0)),
                  pl.BlockSpec((1, f, d), lambda i, t: (j, 0, 0))],
        out_specs=pl.BlockSpec((1, tm, d), lambda i, t: (i, t, 0)),
        out_shape=jax.ShapeDtypeStruct(x.shape, F32),
        compiler_params=_params("arbitrary", "arbitrary"),
    )(x, mod, norm_g, w_gu, w_down)


def _qkv_kernel(x_ref, mod_ref, ng_ref, wk_ref, wqvt_ref, k_ref, qt_ref, vt_ref, *, t):
    x = x_ref[0]
    m = mod_ref[0, 0]
    ng = ng_ref[0]
    d = x.shape[-1]
    h = _norm_mod(x, _row(ng, 0), _row(m, SCALE1), _row(m, SHIFT1)).astype(BF16)
    k_ref[0] = jnp.dot(h, wk_ref[0], preferred_element_type=F32).astype(BF16)
    qvt = lax.dot_general(wqvt_ref[0], h, (((1,), (1,)), ((), ())), preferred_element_type=F32)
    for kk in range(x.shape[0] // t):
        cols = slice(kk * t, (kk + 1) * t)
        qt_ref[0, kk] = (qvt[:d, cols] * (HEAD_DIM ** -0.5 * LOG2E)).astype(BF16)
        vt_ref[0, kk] = qvt[d:, cols].astype(BF16)


def _qkv_layer(x, mod, norm_g, w_k, w_qvt, layer, j, t):
    b, s, d = x.shape
    tm = min(TOKEN_TILE, s)
    transposed = jax.ShapeDtypeStruct((b, s // t, d, t), BF16)
    transposed_spec = pl.BlockSpec((1, tm // t, d, t), lambda i, u: (i, u, 0, 0))
    return pl.pallas_call(
        functools.partial(_qkv_kernel, t=t),
        name="qkv_proj",
        grid=(b, s // tm),
        in_specs=[pl.BlockSpec((1, tm, d), lambda i, u: (i, u, 0)),
                  pl.BlockSpec((1, 1, N_MOD, d), lambda i, u: (layer, i, 0, 0)),
                  pl.BlockSpec((1, 4, d), lambda i, u: (layer, 0, 0)),
                  pl.BlockSpec((1, d, d), lambda i, u: (j, 0, 0)),
                  pl.BlockSpec((1, 2 * d, d), lambda i, u: (j, 0, 0))],
        out_specs=[pl.BlockSpec((1, tm, d), lambda i, u: (i, u, 0)),
                   transposed_spec, transposed_spec],
        out_shape=[jax.ShapeDtypeStruct((b, s, d), BF16), transposed, transposed],
        compiler_params=_params("arbitrary", "arbitrary"),
    )(x, mod, norm_g, w_k, w_qvt)


def _bias_kernel(rb_ref, o_ref, *, t):
    h = pl.program_id(0)
    kind = pl.program_id(1)
    key = lax.broadcasted_iota(jnp.int32, (t, t), 0)
    qry = lax.broadcasted_iota(jnp.int32, (t, t), 1)
    n_signed = kind * t + qry - key
    n = jnp.maximum(n_signed, 0)
    max_exact = N_BUCKETS // 2
    nf = jnp.maximum(n, 1).astype(F32)
    large = max_exact + (jnp.log(nf / max_exact) / math.log(MAX_DISTANCE / max_exact)
                         * (N_BUCKETS - max_exact)).astype(jnp.int32)
    large = jnp.minimum(large, N_BUCKETS - 1)
    bucket = jnp.where(n < max_exact, n, large)
    bias = jnp.zeros((t, t), F32)
    for b in range(N_BUCKETS):
        bias = jnp.where(bucket == b, rb_ref[b * N_HEADS + h], bias)
    bias = jnp.where(n_signed < 0, NEG_BIG, bias * LOG2E)
    o_ref[0, 0] = jnp.where(kind == TILE_MASKED, MASKED_TILE, bias)


def _bias_tiles(rel_bias, t):
    return pl.pallas_call(
        functools.partial(_bias_kernel, t=t),
        name="t5_bias_tiles",
        grid=(N_HEADS, N_TILE_KINDS),
        in_specs=[pl.BlockSpec(memory_space=pltpu.SMEM)],
        out_specs=pl.BlockSpec((1, 1, t, t), lambda h, kind: (h, kind, 0, 0)),
        out_shape=jax.ShapeDtypeStruct((N_HEADS, N_TILE_KINDS, t, t), F32),
        compiler_params=_params("arbitrary", "arbitrary"),
    )(rel_bias.reshape(-1))


def _attn_kernel(qt_ref, k_ref, vt_ref, bias_ref, lv_ref, sg_ref, o_ref,
                 qm, s0, s1, p0, p1, al0, al1, m_sc, acc_sc, *, t, nq, lam_init):
    m_sc[...] = jnp.full(m_sc.shape, NEG_BIG, F32)
    acc_sc[...] = jnp.zeros(acc_sc.shape, F32)
    s1[...] = jnp.full(s1.shape, MASKED_TILE, F32)
    p0[...] = jnp.zeros(p0.shape, BF16)
    al0[...] = jnp.ones(al0.shape, F32)
    ones = jnp.ones((ONES_ROWS, t), BF16)
    q_all = qt_ref[0]
    feat = lax.broadcasted_iota(jnp.int32, q_all.shape, 1)
    zero = jnp.zeros_like(q_all)
    qm[0] = jnp.where(feat < HEAD_DIM, q_all, zero)
    qm[1] = jnp.where(feat >= HEAD_DIM, q_all, zero)
    last = nq - 1

    def advance(pair):
        ki, qi = pair
        wrap = qi == last
        return jnp.where(wrap, ki + 1, ki), jnp.where(wrap, ki + 1, qi + 1)

    def clipped(pair):
        return jnp.minimum(pair[0], last), jnp.minimum(pair[1], last)

    def scores(pair, s_out):
        kl, ql = clipped(pair)
        kind = jnp.where(pair[0] > last, TILE_MASKED, jnp.minimum(ql - kl, TILE_FAR))
        kt = k_ref[0, pl.ds(pl.multiple_of(kl * t, t), t), :]
        bias = bias_ref[0, kind]
        for c in range(2):
            s_out[c] = jnp.dot(kt, qm[c, ql], preferred_element_type=F32) + bias

    def softmax(pair, s_in, p_out, al_out):
        _, ql = clipped(pair)
        for c in range(2):
            sc = s_in[c]
            m_old = m_sc[ql, c]
            m_new = jnp.maximum(m_old, jnp.max(sc, axis=0, keepdims=True))
            p_out[c] = jnp.exp2(sc - m_new).astype(BF16)
            al_out[c] = jnp.exp2(m_old - m_new)
            m_sc[ql, c] = m_new

    def values(pair, p_in, al_in):
        kl, ql = clipped(pair)
        va = jnp.concatenate([vt_ref[0, kl], ones], axis=0)
        for c in range(2):
            acc_sc[ql, c] = (al_in[c] * acc_sc[ql, c]
                             + jnp.dot(va, p_in[c], preferred_element_type=F32))

    def tick_pair(_, carry):
        cur, lag1, lag2 = carry
        scores(cur, s0)
        softmax(lag1, s1, p1, al1)
        values(lag2, p0, al0)
        nxt = advance(cur)
        scores(nxt, s1)
        softmax(cur, s0, p0, al0)
        values(lag1, p1, al1)
        return advance(nxt), nxt, cur

    def tick_group(i, carry):
        for _ in range(TICK_PAIRS_PER_ITER):
            carry = tick_pair(i, carry)
        return carry

    n_ticks = nq * (nq + 1) // 2 + 2
    per_iter = 2 * TICK_PAIRS_PER_ITER
    start = (jnp.int32(0), jnp.int32(0))
    lax.fori_loop(0, (n_ticks + per_iter - 1) // per_iter, tick_group, (start, start, start))

    lv = lv_ref[0]
    lam = (jnp.exp(jnp.sum(_row(lv, 0) * _row(lv, 1), axis=-1, keepdims=True))
           - jnp.exp(jnp.sum(_row(lv, 2) * _row(lv, 3), axis=-1, keepdims=True)) + lam_init)
    gain = sg_ref[0] * (1.0 - lam_init)

    def finish(qi, carry):
        a0 = acc_sc[qi, 0]
        a1 = acc_sc[qi, 1]
        ot = (a0[:V_HEAD_DIM] / a0[V_HEAD_DIM:V_HEAD_DIM + 1]
              - lam * (a1[:V_HEAD_DIM] / a1[V_HEAD_DIM:V_HEAD_DIM + 1]))
        ot = ot * lax.rsqrt(jnp.mean(ot * ot, axis=0, keepdims=True) + EPS)
        o_ref[0, pl.ds(pl.multiple_of(qi * t, t), t), :] = (ot.T * gain).astype(BF16)
        return carry

    lax.fori_loop(0, nq, finish, 0)


def _attention(k, qt, vt, bias, lambda_vecs, subln_g, j, lam_init, t):
    b, s, d = k.shape
    nq = s // t
    assert t >= MAX_DISTANCE and s % t == 0
    score_buf = pltpu.VMEM((2, t, t), F32)
    prob_buf = pltpu.VMEM((2, t, t), BF16)
    alpha_buf = pltpu.VMEM((2, 1, t), F32)
    head_cols = pl.BlockSpec((1, s, V_HEAD_DIM), lambda i, h: (i, 0, h))
    head_rows = pl.BlockSpec((1, nq, V_HEAD_DIM, t), lambda i, h: (i, 0, h, 0))
    return pl.pallas_call(
        functools.partial(_attn_kernel, t=t, nq=nq, lam_init=lam_init),
        name="diff_attention",
        grid=(b, N_HEADS),
        in_specs=[head_rows, head_cols, head_rows,
                  pl.BlockSpec((1, N_TILE_KINDS, t, t), lambda i, h: (h, 0, 0, 0)),
                  pl.BlockSpec((1, 4, HEAD_DIM), lambda i, h: (j, 0, 0)),
                  pl.BlockSpec((1, 1, V_HEAD_DIM), lambda i, h: (j, 0, 0))],
        out_specs=head_cols,
        out_shape=jax.ShapeDtypeStruct((b, s, d), BF16),
        scratch_shapes=[pltpu.VMEM((2, nq, V_HEAD_DIM, t), BF16),
                        score_buf, score_buf, prob_buf, prob_buf, alpha_buf, alpha_buf,
                        pltpu.VMEM((nq, 2, 1, t), F32),
                        pltpu.VMEM((nq, 2, V_HEAD_DIM + ONES_ROWS, t), F32)],
        compiler_params=_params("arbitrary", "arbitrary"),
    )(qt, k, vt, bias, lambda_vecs, subln_g.reshape(-1, 1, V_HEAD_DIM))


def _wo_route_kernel(o_ref, x_ref, mod_ref, ng_ref, wo_ref, wr_ref,
                     xmid_ref, h2_ref, ri_ref, rw_ref, cnt_ref, run_sc, *, tm):
    first = jnp.logical_and(pl.program_id(0) == 0, pl.program_id(1) == 0)

    @pl.when(first)
    def _():
        run_sc[...] = jnp.zeros(run_sc.shape, F32)

    x = x_ref[0]
    m = mod_ref[0, 0]
    ng = ng_ref[0]
    y = jnp.dot(o_ref[0], wo_ref[0], preferred_element_type=F32)
    xn = x + _row(m, GATE1) * _rms(y, _row(ng, 1))
    xmid_ref[0] = xn
    h2 = _norm_mod(xn, _row(ng, 2), _row(m, SCALE2), _row(m, SHIFT2))
    h2_ref[0] = h2

    h_hi = h2.astype(BF16)
    h_lo = (h2 - h_hi.astype(F32)).astype(BF16)
    hi_both = jnp.dot(h_hi, wr_ref[0], preferred_element_type=F32)
    logits = (hi_both[:, :LANES] + hi_both[:, LANES:]
              + jnp.dot(h_lo, wr_ref[0, :, :LANES], preferred_element_type=F32))
    lane = lax.broadcasted_iota(jnp.int32, logits.shape, 1)
    lane_f = lane.astype(F32)
    neg_inf = jnp.float32(-jnp.inf)
    logits = jnp.where(lane < N_EXPERTS, logits, neg_inf)
    m1 = jnp.max(logits, axis=-1, keepdims=True)
    i1 = jnp.min(jnp.where(logits == m1, lane_f, float(LANES)), axis=-1, keepdims=True)
    oh1 = lane_f == i1
    rest = jnp.where(oh1, neg_inf, logits)
    m2 = jnp.max(rest, axis=-1, keepdims=True)
    i2 = jnp.min(jnp.where(rest == m2, lane_f, float(LANES)), axis=-1, keepdims=True)
    oh2 = lane_f == i2
    i1 = i1.astype(jnp.int32)
    i2 = i2.astype(jnp.int32)
    e21 = jnp.exp(m2 - m1)
    w1 = 1.0 / (1.0 + e21)
    w2 = e21 * w1

    oh = jnp.where(jnp.logical_or(oh1, oh2), 1.0, 0.0)
    r = lax.broadcasted_iota(jnp.int32, (tm, tm), 0)
    c = lax.broadcasted_iota(jnp.int32, (tm, tm), 1)
    tri = jnp.where(r > c, 1.0, 0.0).astype(BF16)
    prefix = jnp.dot(tri, oh.astype(BF16), preferred_element_type=F32) + run_sc[...]
    r1 = jnp.sum(jnp.where(oh1, prefix, 0.0), axis=-1, keepdims=True).astype(jnp.int32)
    r2 = jnp.sum(jnp.where(oh2, prefix, 0.0), axis=-1, keepdims=True).astype(jnp.int32)
    run_sc[...] = run_sc[...] + jnp.sum(oh, axis=0, keepdims=True)

    ri_ref[0] = jnp.where(lane == 0, i1, jnp.where(lane == 1, i2,
                          jnp.where(lane == 2, r1, jnp.where(lane == 3, r2, 0))))
    rw_ref[0] = jnp.where(lane == 0, w1, jnp.where(lane == 1, w2, 0.0))
    cnt_ref[...] = run_sc[...]


def _wo_route_layer(o, x, mod, norm_g, w_o, w_router, layer, j):
    b, s, d = x.shape
    tm = min(TOKEN_TILE, s)
    act = jax.ShapeDtypeStruct((b, s, d), F32)
    tile = pl.BlockSpec((1, tm, d), lambda i, t: (i, t, 0))
    lanes = pl.BlockSpec((1, tm, LANES), lambda i, t: (i, t, 0))
    return pl.pallas_call(
        functools.partial(_wo_route_kernel, tm=tm),
        name="attn_out_router",
        grid=(b, s // tm),
        in_specs=[tile, tile,
                  pl.BlockSpec((1, 1, N_MOD, d), lambda i, t: (layer, i, 0, 0)),
                  pl.BlockSpec((1, 4, d), lambda i, t: (layer, 0, 0)),
                  pl.BlockSpec((1, d, d), lambda i, t: (j, 0, 0)),
                  pl.BlockSpec((1, d, 2 * LANES), lambda i, t: (j, 0, 0))],
        out_specs=[tile, tile, lanes, lanes, pl.BlockSpec((1, LANES), lambda i, t: (0, 0))],
        out_shape=[act, act,
                   jax.ShapeDtypeStruct((b, s, LANES), jnp.int32),
                   jax.ShapeDtypeStruct((b, s, LANES), F32),
                   jax.ShapeDtypeStruct((1, LANES), F32)],
        scratch_shapes=[pltpu.VMEM((1, LANES), F32)],
        compiler_params=_params("arbitrary", "arbitrary"),
    )(o, x, mod, norm_g, w_o, w_router)


def _row_copy(src, src_row, dst, dst_row, sem):
    return pltpu.make_async_copy(src.at[pl.ds(src_row, 1)], dst.at[pl.ds(dst_row, 1)], sem)


ROW_UNROLL = 8
COMBINE_PARTS = 4


def _scatter_kernel(offs_ref, dst_ref, h_ref, xs_ref, zbuf, sem, *, tm, tg):
    @pl.when(pl.program_id(0) == 0)
    def _():
        zbuf[...] = jnp.zeros(zbuf.shape, F32)
        for e in range(N_EXPERTS):
            @pl.when(offs_ref[e + 1] > offs_ref[e])
            def _():
                start = pl.multiple_of(offs_ref[e + 1] - tg, tg)
                cp = pltpu.make_async_copy(zbuf, xs_ref.at[pl.ds(start, tg)], sem)
                cp.start()
                cp.wait()

    def issue(g, carry):
        for u in range(ROW_UNROLL):
            t = g * ROW_UNROLL + u
            for k in range(TOP_K):
                _row_copy(h_ref, t, xs_ref, dst_ref[TOP_K * t + k], sem).start(priority=k)
        return carry

    lax.fori_loop(0, tm // ROW_UNROLL, issue, 0)

    def drain(g, carry):
        for _ in range(TOP_K * ROW_UNROLL):
            _row_copy(h_ref, 0, xs_ref, 0, sem).wait()
        return carry

    lax.fori_loop(0, tm // ROW_UNROLL, drain, 0)


def _scatter_rows(offs, dest_flat, h2, rows, tg):
    n, d = h2.shape
    tm = min(ROW_TILE, n)
    return pl.pallas_call(
        functools.partial(_scatter_kernel, tm=tm, tg=tg),
        name="moe_group_rows",
        grid_spec=pltpu.PrefetchScalarGridSpec(
            num_scalar_prefetch=1,
            grid=(n // tm,),
            in_specs=[pl.BlockSpec((TOP_K * tm,), lambda i, offs: (i,), memory_space=pltpu.SMEM),
                      pl.BlockSpec((tm, d), lambda i, offs: (i, 0))],
            out_specs=pl.BlockSpec(memory_space=pl.ANY),
            scratch_shapes=[pltpu.VMEM((tg, d), F32), pltpu.SemaphoreType.DMA(())]),
        out_shape=jax.ShapeDtypeStruct((rows, d), F32),
        compiler_params=_params("arbitrary"),
    )(offs, dest_flat, h2)


def _gmm_kernel(te_ref, nu_ref, x_ref, wg_ref, wu_ref, wd_ref, o_ref, xb, acc, *, sub):
    i = pl.program_id(0)
    jf = pl.program_id(1)

    @pl.when(i < nu_ref[0])
    def _():
        @pl.when(jf == 0)
        def _():
            xb[...] = x_ref[...].astype(BF16)
            acc[...] = jnp.zeros(acc.shape, F32)

        x = xb[...]
        for c in range(wd_ref.shape[0] // sub):
            cols = slice(c * sub, (c + 1) * sub)
            gate = jnp.dot(x, wg_ref[:, cols], preferred_element_type=F32)
            up = jnp.dot(x, wu_ref[:, cols], preferred_element_type=F32)
            hm = (_silu(gate) * up).astype(BF16)
            acc[...] += jnp.dot(hm, wd_ref[cols, :], preferred_element_type=F32)

        @pl.when(jf == pl.num_programs(1) - 1)
        def _():
            o_ref[...] = acc[...]


def _expert_ffn(tile_expert, n_used, xs, w_gu, w_down, tg, j):
    rows, d = xs.shape
    f = w_down.shape[2]
    sub = FF_CHUNK if f % FF_CHUNK == 0 else f
    fc = f // 2 if f % (2 * sub) == 0 else f
    nf = f // fc

    def row_map(i, jf, te, nu):
        return (jnp.minimum(i, nu[0] - 1), 0)

    def w_map(col0):
        def index_map(i, jf, te, nu):
            last = nu[0] - 1
            return (j, te[jnp.minimum(i, last)], 0, col0 + jnp.where(i <= last, jf, nf - 1))
        return index_map

    def wd_map(i, jf, te, nu):
        last = nu[0] - 1
        return (j, te[jnp.minimum(i, last)], jnp.where(i <= last, jf, nf - 1), 0)

    return pl.pallas_call(
        functools.partial(_gmm_kernel, sub=sub),
        name="moe_expert_swiglu",
        grid_spec=pltpu.PrefetchScalarGridSpec(
            num_scalar_prefetch=2,
            grid=(rows // tg, nf),
            in_specs=[pl.BlockSpec((tg, d), row_map),
                      pl.BlockSpec((None, None, d, fc), w_map(0)),
                      pl.BlockSpec((None, None, d, fc), w_map(nf)),
                      pl.BlockSpec((None, None, fc, d), wd_map)],
            out_specs=pl.BlockSpec((tg, d), row_map),
            scratch_shapes=[pltpu.VMEM((tg, d), BF16), pltpu.VMEM((tg, d), F32)]),
        out_shape=jax.ShapeDtypeStruct((rows, d), F32),
        compiler_params=_params("arbitrary", "arbitrary"),
    )(tile_expert, n_used, xs, w_gu, w_gu, w_down)


def _combine_kernel(src_ref, rw_ref, x_ref, mod_ref, ng_ref, ys_ref, o_ref, ybuf, sems, *, tm):
    part = tm // COMBINE_PARTS
    groups = part // ROW_UNROLL

    def issue(g, carry, sem):
        for u in range(ROW_UNROLL):
            t = g * ROW_UNROLL + u
            for k in range(TOP_K):
                _row_copy(ys_ref, src_ref[TOP_K * t + k], ybuf.at[k], t, sem).start(priority=k)
        return carry

    def drain(g, carry, sem):
        for _ in range(TOP_K * ROW_UNROLL):
            _row_copy(ys_ref, 0, ybuf.at[0], 0, sem).wait()
        return carry

    for pi in range(COMBINE_PARTS):
        lax.fori_loop(pi * groups, (pi + 1) * groups,
                      functools.partial(issue, sem=sems.at[pi]), 0)

    m = mod_ref[0, 0]
    ng = ng_ref[0]
    for pi in range(COMBINE_PARTS):
        lax.fori_loop(0, groups, functools.partial(drain, sem=sems.at[pi]), 0)
        rows = slice(pi * part, (pi + 1) * part)
        rw = rw_ref[0, rows, :]
        y = rw[:, 0:1] * ybuf[0, rows, :] + rw[:, 1:2] * ybuf[1, rows, :]
        o_ref[0, rows, :] = x_ref[0, rows, :] + _row(m, GATE2) * _rms(y, _row(ng, 3))


def _combine_layer(dest_flat, rw, x, mod, norm_g, ys, layer):
    b, s, d = x.shape
    tm = min(ROW_TILE, s)
    nt = s // tm
    return pl.pallas_call(
        functools.partial(_combine_kernel, tm=tm),
        name="moe_combine",
        grid=(b, nt),
        in_specs=[pl.BlockSpec((TOP_K * tm,), lambda i, t: (i * nt + t,), memory_space=pltpu.SMEM),
                  pl.BlockSpec((1, tm, LANES), lambda i, t: (i, t, 0)),
                  pl.BlockSpec((1, tm, d), lambda i, t: (i, t, 0)),
                  pl.BlockSpec((1, 1, N_MOD, d), lambda i, t: (layer, i, 0, 0)),
                  pl.BlockSpec((1, 4, d), lambda i, t: (layer, 0, 0)),
                  pl.BlockSpec(memory_space=pl.ANY)],
        out_specs=pl.BlockSpec((1, tm, d), lambda i, t: (i, t, 0)),
        scratch_shapes=[pltpu.VMEM((TOP_K, tm, d), F32),
                        pltpu.SemaphoreType.DMA((COMBINE_PARTS,))],
        out_shape=jax.ShapeDtypeStruct(x.shape, F32),
        compiler_params=_params("arbitrary", "arbitrary"),
    )(dest_flat, rw, x, mod, norm_g, ys)


def _moe_layer(h2, ri, rw, counts, xmid, mod, norm_g, w_gu, w_down, layer, j):
    b, s, d = xmid.shape
    n = b * s
    tg = min(ROW_TILE, n)
    cnt = counts[0, :N_EXPERTS].astype(jnp.int32)
    padded = (cnt + tg - 1) // tg * tg
    offs = jnp.concatenate([jnp.zeros((1,), jnp.int32), jnp.cumsum(padded)]).astype(jnp.int32)
    n_tiles = 2 * n // tg + N_EXPERTS
    starts = jnp.arange(n_tiles, dtype=jnp.int32) * tg
    tile_expert = jnp.minimum(jnp.sum(starts[:, None] >= offs[None, 1:], axis=1),
                              N_EXPERTS - 1).astype(jnp.int32)
    n_used = (offs[N_EXPERTS:] // tg).astype(jnp.int32)
    sel = ri[:, :, 0:2]
    rank = ri[:, :, 2:4]
    starts_of = jnp.sum(jnp.where(sel[..., None] == jnp.arange(N_EXPERTS, dtype=jnp.int32),
                                  offs[:N_EXPERTS], 0), axis=-1)
    dest_flat = (starts_of + rank).reshape(-1).astype(jnp.int32)

    xs = _scatter_rows(offs, dest_flat, h2.reshape(n, d), n_tiles * tg, tg)
    ys = _expert_ffn(tile_expert, n_used, xs, w_gu, w_down, tg, j)
    return _combine_layer(dest_flat, rw, xmid, mod, norm_g, ys, layer)


def kernel(x, c, w_mod, b_mod, norm_g, pool_w, pool_scale, w_qkv, w_o, subln_g, lambda_vecs,
           rel_bias, ffn_w_gu, ffn_w_down, w_router, moe_w_gu, moe_w_down):
    depth = w_mod.shape[0]
    s = x.shape[1]
    d = x.shape[2]
    assert d == 2 * N_HEADS * HEAD_DIM
    t_attn = min(ATTN_TILE, s)

    mod = _modulation(c, w_mod, b_mod)
    pool_w16 = pool_w.astype(BF16)
    ffn_gu16 = ffn_w_gu.astype(BF16)
    ffn_down16 = ffn_w_down.astype(BF16)
    wk16 = w_qkv[:, :, d:2 * d].astype(BF16)
    wqvt16 = jnp.swapaxes(jnp.concatenate([w_qkv[:, :, :d], w_qkv[:, :, 2 * d:]], axis=2),
                          1, 2).astype(BF16)
    wo16 = w_o.astype(BF16)
    moe_gu16 = moe_w_gu.astype(BF16)
    moe_down16 = moe_w_down.astype(BF16)
    router_pad = jnp.pad(w_router, ((0, 0), (0, 0), (0, LANES - N_EXPERTS)))
    router_hi = router_pad.astype(BF16)
    router_lo = (router_pad - router_hi.astype(F32)).astype(BF16)
    router_split = jnp.concatenate([router_hi, router_lo], axis=2)
    bias = _bias_tiles(rel_bias, t_attn)

    for i in range(depth):
        j = i // 2
        if i % 2 == 0:
            x = _pool_layer(x, mod, norm_g, pool_w16, pool_scale, i, j)
            x = _ffn_layer(x, mod, norm_g, ffn_gu16, ffn_down16, i, j)
        else:
            lam_init = 0.8 - 0.6 * math.exp(-0.3 * i)
            k, qt, vt = _qkv_layer(x, mod, norm_g, wk16, wqvt16, i, j, t_attn)
            o = _attention(k, qt, vt, bias, lambda_vecs, subln_g, j, lam_init, t_attn)
            xmid, h2, ri, rw, counts = _wo_route_layer(o, x, mod, norm_g, wo16, router_split, i, j)
            x = _moe_layer(h2, ri, rw, counts, xmid, mod, norm_g, moe_gu16, moe_down16, i, j)
    return x
```
